```python
import jax, jax.numpy as jnp
from jax import lax
import numpy as np

D_MODEL = 1024
BATCH = 32
SEQ = 2048
DEPTH = 1

HEAD_DIM = 64
N_HEADS = D_MODEL // HEAD_DIM
N_HEADS_A = N_HEADS // 2
N_HEADS_B = N_HEADS - N_HEADS_A
WIDTH_A = N_HEADS_A * HEAD_DIM
WIDTH_B = N_HEADS_B * HEAD_DIM
DILATED_PATTERNS = ((128, 1), (512, 4), (2048, 16))
Q_BLOCK = 128
ROT_DIM = HEAD_DIM // 4
ROPE_THETA = 500000.0
D_FF = ((8 * D_MODEL // 3 + 63) // 64) * 64
N_MOD = 9
EPS = 1e-6
ATTN_SCALE = HEAD_DIM ** -0.5
NEG = -1e30
COL_SIZES = (WIDTH_A, WIDTH_A, WIDTH_A, WIDTH_B, WIDTH_B, WIDTH_B, N_HEADS_B)
COL_OFFSETS = tuple(int(o) for o in np.cumsum(COL_SIZES)[:-1])
IN_COLS = int(sum(COL_SIZES))

kernel_name = 'hybrid_dilated_fox_macaron_block'


def rmsnorm(x, g):
    xf = x.astype(jnp.float32)
    y = xf * lax.rsqrt(jnp.mean(xf * xf, axis=-1, keepdims=True) + EPS)
    return (y * g.astype(jnp.float32)).astype(x.dtype)


def partial_rotary(t, positions):
    inv_freq = ROPE_THETA ** (-jnp.arange(0, ROT_DIM, 2, dtype=jnp.float32) / ROT_DIM)
    ang = positions.astype(jnp.float32)[:, None, :, None] * inv_freq
    cos, sin = jnp.cos(ang), jnp.sin(ang)
    tf = t.astype(jnp.float32)
    x1 = tf[..., :ROT_DIM // 2]
    x2 = tf[..., ROT_DIM // 2:ROT_DIM]
    rot = jnp.concatenate([x1 * cos - x2 * sin, x2 * cos + x1 * sin, tf[..., ROT_DIM:]], axis=-1)
    return rot.astype(t.dtype)


def swiglu(h, w_gate, w_up, w_down):
    return (jax.nn.silu(h @ w_gate) * (h @ w_up)) @ w_down


def banded_causal_attention(q, k, v, w):
    L = q.shape[-2]
    lead = q.shape[:-2]
    nb = -(-L // Q_BLOCK)
    Lp = nb * Q_BLOCK
    pad = [(0, 0)] * (q.ndim - 2)
    qp = jnp.pad(q, pad + [(0, Lp - L), (0, 0)])
    kp = jnp.pad(k, pad + [(w, Lp - L), (0, 0)])
    vp = jnp.pad(v, pad + [(w, Lp - L), (0, 0)])
    span = Q_BLOCK + w
    q_blk = qp.reshape(*lead, nb, Q_BLOCK, HEAD_DIM)
    idx = jnp.arange(nb)[:, None] * Q_BLOCK + jnp.arange(span)[None, :]
    k_blk = jnp.take(kp, idx, axis=-2)
    v_blk = jnp.take(vp, idx, axis=-2)
    s = jnp.einsum('...nqd,...nkd->...nqk', q_blk, k_blk,
                   preferred_element_type=jnp.float32) * ATTN_SCALE
    dist = jnp.arange(Q_BLOCK)[:, None] + w - jnp.arange(span)[None, :]
    key_pos = idx - w
    valid = ((dist >= 0) & (dist <= w))[None] & (key_pos >= 0)[:, None, :]
    s = jnp.where(valid, s, NEG)
    lse = jax.nn.logsumexp(s, axis=-1)
    p = jnp.exp(s - lse[..., None])
    o = jnp.einsum('...nqk,...nkd->...nqd', p.astype(v.dtype), v_blk)
    o = o.reshape(*lead, Lp, HEAD_DIM)[..., :L, :]
    lse = lse.reshape(*lead, Lp)[..., :L]
    return o, lse


def dilated_mixture_attention(q, k, v):
    B, H, S, hd = q.shape
    outs, lses = [], []
    for window, d in DILATED_PATTERNS:
        w_sub = window // d
        to_cls = lambda t: t.reshape(B, H, S // d, d, hd).swapaxes(2, 3)
        o, lse = banded_causal_attention(to_cls(q), to_cls(k), to_cls(v), w_sub)
        outs.append(o.swapaxes(2, 3).reshape(B, H, S, hd))
        lses.append(lse.swapaxes(2, 3).reshape(B, H, S))
    alpha = jax.nn.softmax(jnp.stack(lses, axis=0), axis=0)
    return jnp.einsum('pbhs,pbhsd->bhsd', alpha.astype(q.dtype), jnp.stack(outs, axis=0))


def forgetting_attention(q, k, v, f_logit):
    S = q.shape[2]
    log_f = jax.nn.log_sigmoid(f_logit.astype(jnp.float32)).transpose(0, 2, 1)
    F = lax.cumsum(log_f, axis=2)
    outs = []
    for i in range(S // Q_BLOCK):
        lo, hi = i * Q_BLOCK, (i + 1) * Q_BLOCK
        s = jnp.einsum('bhqd,bhkd->bhqk', q[:, :, lo:hi], k[:, :, :hi],
                       preferred_element_type=jnp.float32) * ATTN_SCALE
        s = s + F[:, :, lo:hi, None] - F[:, :, None, :hi]
        causal = (lo + jnp.arange(Q_BLOCK))[:, None] >= jnp.arange(hi)[None, :]
        p = jax.nn.softmax(jnp.where(causal, s, NEG), axis=-1)
        outs.append(jnp.einsum('bhqk,bhkd->bhqd', p.astype(v.dtype), v[:, :, :hi]))
    return jnp.concatenate(outs, axis=2)


def hybrid_mixer(h, positions, w_in, b_forget, g_out_a, g_out_b, w_out):
    B, S, _ = h.shape
    proj = h @ w_in
    qa, ka, va, qb, kb, vb, f_logit = jnp.split(proj, COL_OFFSETS, axis=-1)
    heads = lambda t, n: t.reshape(B, S, n, HEAD_DIM).transpose(0, 2, 1, 3)
    qa = partial_rotary(heads(qa, N_HEADS_A), positions)
    ka = partial_rotary(heads(ka, N_HEADS_A), positions)
    out_a = dilated_mixture_attention(qa, ka, heads(va, N_HEADS_A))
    out_b = forgetting_attention(heads(qb, N_HEADS_B), heads(kb, N_HEADS_B), heads(vb, N_HEADS_B),
                                 f_logit + b_forget)
    flat = lambda t: t.transpose(0, 2, 1, 3).reshape(B, S, -1)
    merged = jnp.concatenate([rmsnorm(flat(out_a), g_out_a), rmsnorm(flat(out_b), g_out_b)], axis=-1)
    return merged @ w_out


def setup_inputs(seed: int = 0) -> dict:
    key = jax.random.key(seed)
    ks = jax.random.split(key, 24)
    nrm = lambda k, shape, s: jax.random.normal(k, shape, jnp.float32) * s
    gain = lambda k, n: 1.0 + 0.05 * jax.random.normal(k, (DEPTH, n), jnp.float32)
    return {
        'x': nrm(ks[0], (BATCH, SEQ, D_MODEL), 1.0),
        'c': nrm(ks[1], (BATCH, D_MODEL), 1.0),
        'positions': jnp.broadcast_to(jnp.arange(SEQ, dtype=jnp.int32)[None, :], (BATCH, SEQ)),
        'w_ada': nrm(ks[2], (DEPTH, D_MODEL, N_MOD * D_MODEL), 0.01),
        'b_ada': nrm(ks[3], (DEPTH, N_MOD * D_MODEL), 0.02),
        'g_pre_ff1': gain(ks[4], D_MODEL),
        'g_post_ff1': gain(ks[5], D_MODEL),
        'w_ff1_gate': nrm(ks[6], (DEPTH, D_MODEL, D_FF), D_MODEL ** -0.5),
        'w_ff1_up': nrm(ks[7], (DEPTH, D_MODEL, D_FF), D_MODEL ** -0.5),
        'w_ff1_down': nrm(ks[8], (DEPTH, D_FF, D_MODEL), D_FF ** -0.5),
        'g_pre_mix': gain(ks[9], D_MODEL),
        'g_post_mix': gain(ks[10], D_MODEL),
        'w_in': nrm(ks[11], (DEPTH, D_MODEL, IN_COLS), D_MODEL ** -0.5),
        'b_forget': jax.random.uniform(ks[12], (DEPTH, N_HEADS_B), jnp.float32, 1.0, 4.0),
        'g_out_a': gain(ks[13], WIDTH_A),
        'g_out_b': gain(ks[14], WIDTH_B),
        'w_out': nrm(ks[15], (DEPTH, D_MODEL, D_MODEL), D_MODEL ** -0.5),
        'g_pre_ff2': gain(ks[16], D_MODEL),
        'g_post_ff2': gain(ks[17], D_MODEL),
        'w_ff2_gate': nrm(ks[18], (DEPTH, D_MODEL, D_FF), D_MODEL ** -0.5),
        'w_ff2_up': nrm(ks[19], (DEPTH, D_MODEL, D_FF), D_MODEL ** -0.5),
        'w_ff2_down': nrm(ks[20], (DEPTH, D_FF, D_MODEL), D_FF ** -0.5),
    }


def reference(x, c, positions, w_ada, b_ada, g_pre_ff1, g_post_ff1, w_ff1_gate, w_ff1_up, w_ff1_down,
              g_pre_mix, g_post_mix, w_in, b_forget, g_out_a, g_out_b, w_out,
              g_pre_ff2, g_post_ff2, w_ff2_gate, w_ff2_up, w_ff2_down):
    B = x.shape[0]
    silu_c = jax.nn.silu(c)
    for l in range(DEPTH):
        mod = (silu_c @ w_ada[l] + b_ada[l]).reshape(B, N_MOD, D_MODEL)
        m = lambda i: mod[:, i][:, None, :]
        h = rmsnorm(x, g_pre_ff1[l]) * (1.0 + m(1)) + m(0)
        y = rmsnorm(swiglu(h, w_ff1_gate[l], w_ff1_up[l], w_ff1_down[l]), g_post_ff1[l])
        x = x + 0.5 * m(2) * y
        h = rmsnorm(x, g_pre_mix[l]) * (1.0 + m(4)) + m(3)
        y = rmsnorm(hybrid_mixer(h, positions, w_in[l], b_forget[l], g_out_a[l], g_out_b[l], w_out[l]),
                    g_post_mix[l])
        x = x + m(5) * y
        h = rmsnorm(x, g_pre_ff2[l]) * (1.0 + m(7)) + m(6)
        y = rmsnorm(swiglu(h, w_ff2_gate[l], w_ff2_up[l], w_ff2_down[l]), g_post_ff2[l])
        x = x + 0.5 * m(8) * y
    return x
```

```python
import functools

import numpy as np
import jax
import jax.numpy as jnp
from jax import lax
from jax.experimental import pallas as pl
from jax.experimental.pallas import tpu as pltpu

F32 = jnp.float32
BF16 = jnp.bfloat16

D_MODEL = 1024
HEAD_DIM = 64
N_HEADS_A = 8
N_HEADS_B = 8
WIDTH_A = N_HEADS_A * HEAD_DIM
WIDTH_B = N_HEADS_B * HEAD_DIM
ROT_DIM = HEAD_DIM // 4
ROPE_THETA = 500000.0
D_FF = 2752
N_MOD = 9
EPS = 1e-6
ATTN_SCALE = HEAD_DIM ** -0.5
NEG = -1e30
QKV_COLS = 3 * WIDTH_A + 3 * WIDTH_B

LANES = 128
FF_CHUNK = 256
D_FF_PAD = -(-D_FF // FF_CHUNK) * FF_CHUNK
N_FF_CHUNKS = D_FF_PAD // FF_CHUNK
BLK = 128
VMEM_LIMIT = 56 * 1024 * 1024


def _rms(x, g):
    ms = jnp.mean(x * x, axis=-1, keepdims=True)
    return x * lax.rsqrt(ms + EPS) * g


def _split3(x):
    hi = x.astype(BF16)
    r1 = x - hi.astype(F32)
    mid = r1.astype(BF16)
    lo = (r1 - mid.astype(F32)).astype(BF16)
    return hi, mid, lo


def _const_spec(shape):
    nd = len(shape)
    return pl.BlockSpec(shape, lambda *_: (0,) * nd, pipeline_mode=pl.Buffered(1))


def _ada_kernel(c_ref, w_ref, b_ref, o_ref):
    c = c_ref[...]
    sc = (c * jax.nn.sigmoid(c)).astype(BF16)
    o_ref[...] = jnp.dot(sc, w_ref[...].astype(BF16), preferred_element_type=F32) + b_ref[...]


def _ada_call(c, w_ada, b_ada):
    B, D = c.shape
    N = w_ada.shape[1]
    tn = 1024
    return pl.pallas_call(
        _ada_kernel,
        grid=(N // tn,),
        in_specs=[pl.BlockSpec((B, D), lambda j: (0, 0)),
                  pl.BlockSpec((D, tn), lambda j: (0, j)),
                  pl.BlockSpec((1, tn), lambda j: (0, j))],
        out_specs=pl.BlockSpec((B, tn), lambda j: (0, j)),
        out_shape=jax.ShapeDtypeStruct((B, N), F32),
        compiler_params=pltpu.CompilerParams(dimension_semantics=("arbitrary",),
                                             vmem_limit_bytes=VMEM_LIMIT),
        name="adaln",
    )(c, w_ada, b_ada.reshape(1, N))


def _ffn_kernel(x_ref, mod_ref, gpre_ref, gpost_ref, wg_ref, wu_ref, wd_ref, o_ref, acc_ref, *, row0):
    x = x_ref[...]
    shift = mod_ref[0, row0:row0 + 1, :]
    scale = mod_ref[0, row0 + 1:row0 + 2, :]
    gate = mod_ref[0, row0 + 2:row0 + 3, :]
    hb = (_rms(x, gpre_ref[...]) * (1.0 + scale) + shift).astype(BF16)
    acc_ref[...] = jnp.zeros_like(acc_ref)

    def chunk(c, carry):
        g = jnp.dot(hb, wg_ref[c], preferred_element_type=F32)
        u = jnp.dot(hb, wu_ref[c], preferred_element_type=F32)
        a = (g * jax.nn.sigmoid(g) * u).astype(BF16)
        acc_ref[...] += jnp.dot(a, wd_ref[c], preferred_element_type=F32)
        return carry

    lax.fori_loop(0, N_FF_CHUNKS, chunk, 0)
    y = _rms(acc_ref[...], gpost_ref[...])
    o_ref[...] = x + (0.5 * gate) * y


def _ffn_call(x2d, mod, g_pre, g_post, wg, wu, wd, *, row0, seq, tm):
    T, D = x2d.shape
    per_seq = seq // tm
    kern = functools.partial(_ffn_kernel, row0=row0)
    return pl.pallas_call(
        kern,
        grid=(T // tm,),
        in_specs=[pl.BlockSpec((tm, D), lambda i: (i, 0)),
                  pl.BlockSpec((1, N_MOD, D), lambda i: (i // per_seq, 0, 0)),
                  _const_spec((1, D)), _const_spec((1, D)),
                  _const_spec(wg.shape), _const_spec(wu.shape), _const_spec(wd.shape)],
        out_specs=pl.BlockSpec((tm, D), lambda i: (i, 0)),
        out_shape=jax.ShapeDtypeStruct((T, D), F32),
        scratch_shapes=[pltpu.VMEM((tm, D), F32)],
        compiler_params=pltpu.CompilerParams(dimension_semantics=("arbitrary",),
                                             vmem_limit_bytes=VMEM_LIMIT),
        name="ffn_row%d" % row0,
    )(x2d, mod, g_pre, g_post, wg, wu, wd)


def _rope_consts():
    half = ROT_DIM // 2
    e = np.zeros((3, 2 * half, 3 * LANES), np.float32)
    c = np.zeros((1, 3 * LANES), np.float32)
    for lane in range(LANES):
        d = lane % HEAD_DIM
        if d < ROT_DIM:
            e[:, d % half, lane] = 1.0
        else:
            c[0, lane] = 1.0
        if d < half:
            e[:, half + d, LANES + lane] = -1.0
        elif d < ROT_DIM:
            e[:, half + d - half, 2 * LANES + lane] = 1.0
    return e.reshape(3 * 2 * half, 3 * LANES), c


def _proj_kernel(x_ref, mod_ref, g_ref, pos_ref, invf_ref, wqkv_ref, wf_ref, bf_ref, e_ref, ec_ref, tri_ref,
                 qa_ref, ka_ref, va_ref, qb_ref, kb_ref, vb_ref, fcol_ref, frow_ref, carry_ref):
    j = pl.program_id(1)
    tm = x_ref.shape[0]
    x = x_ref[...]
    shift = mod_ref[0, 3:4, :]
    scale = mod_ref[0, 4:5, :]
    hb = (_rms(x, g_ref[...]) * (1.0 + scale) + shift).astype(BF16)
    proj = jnp.dot(hb, wqkv_ref[...], preferred_element_type=F32)

    posf = pos_ref[0].astype(F32)
    ang = invf_ref[...] * posf
    cs = jnp.concatenate([jnp.cos(ang), jnp.sin(ang)], axis=0)
    cs3 = jnp.concatenate(_split3(cs), axis=0)
    tabs = lax.dot_general(cs3, e_ref[...], (((0,), (0,)), ((), ())),
                           preferred_element_type=F32) + ec_ref[...]
    t_cos = tabs[:, 0:LANES]
    t_s1 = tabs[:, LANES:2 * LANES]
    t_s2 = tabs[:, 2 * LANES:3 * LANES]
    half = ROT_DIM // 2

    def rope(t):
        return t * t_cos + pltpu.roll(t, LANES - half, 1) * t_s1 + pltpu.roll(t, half, 1) * t_s2

    for i in range(WIDTH_A // LANES):
        lo = i * LANES
        qa_ref[:, lo:lo + LANES] = rope(proj[:, lo:lo + LANES])
        ka_ref[:, lo:lo + LANES] = rope(proj[:, WIDTH_A + lo:WIDTH_A + lo + LANES])
    va_ref[...] = proj[:, 2 * WIDTH_A:3 * WIDTH_A]
    o = 3 * WIDTH_A
    qb_ref[...] = proj[:, o:o + WIDTH_B].astype(BF16)
    kb_ref[...] = proj[:, o + WIDTH_B:o + 2 * WIDTH_B].astype(BF16)
    vb_ref[...] = proj[:, o + 2 * WIDTH_B:o + 3 * WIDTH_B].astype(BF16)

    fl = jnp.dot(hb, wf_ref[...], preferred_element_type=F32) + bf_ref[...]
    lf = jnp.minimum(fl, 0.0) - jnp.log1p(jnp.exp(-jnp.abs(fl)))
    hi, mid, lo3 = _split3(lf)
    lane = lax.broadcasted_iota(jnp.int32, lf.shape, 1)
    zero = jnp.zeros_like(hi)
    lf3 = jnp.where(lane < 8, hi, jnp.where(lane < 16, mid, jnp.where(lane < 24, lo3, zero)))
    cum = jnp.dot(tri_ref[...], lf3, preferred_element_type=F32)
    cum = cum + pltpu.roll(cum, LANES - 8, 1) + pltpu.roll(cum, LANES - 16, 1)

    @pl.when(j == 0)
    def _():
        carry_ref[...] = jnp.zeros_like(carry_ref)

    f_tot = cum + carry_ref[...]
    carry_ref[...] = f_tot[tm - 1:tm, :]
    fcol_ref[...] = f_tot
    frow_ref[...] = jnp.transpose(f_tot)[0:N_HEADS_B, :]


def _proj_call(x2d, mod, g_pre, pos3, invf, wqkv, wf, bfg, e, ec, tri, *, batch, seq, tm):
    T, D = x2d.shape
    per_seq = seq // tm
    tok = lambda b, j: (b * per_seq + j, 0)
    out_shape = ([jax.ShapeDtypeStruct((T, WIDTH_A), F32)] * 3
                 + [jax.ShapeDtypeStruct((T, WIDTH_B), BF16)] * 3
                 + [jax.ShapeDtypeStruct((T, LANES), F32),
                    jax.ShapeDtypeStruct((batch, N_HEADS_B, seq), F32)])
    out_specs = ([pl.BlockSpec((tm, WIDTH_A), tok)] * 3 + [pl.BlockSpec((tm, WIDTH_B), tok)] * 3
                 + [pl.BlockSpec((tm, LANES), tok),
                    pl.BlockSpec((None, N_HEADS_B, tm), lambda b, j: (b, 0, j))])
    return pl.pallas_call(
        _proj_kernel,
        grid=(batch, per_seq),
        in_specs=[pl.BlockSpec((tm, D), tok),
                  pl.BlockSpec((1, N_MOD, D), lambda b, j: (b, 0, 0)),
                  _const_spec((1, D)),
                  pl.BlockSpec((None, 1, tm), lambda b, j: (b, 0, j)),
                  _const_spec(invf.shape), _const_spec(wqkv.shape), _const_spec(wf.shape),
                  _const_spec(bfg.shape), _const_spec(e.shape), _const_spec(ec.shape),
                  _const_spec(tri.shape)],
        out_specs=out_specs,
        out_shape=out_shape,
        scratch_shapes=[pltpu.VMEM((1, LANES), F32)],
        compiler_params=pltpu.CompilerParams(dimension_semantics=("arbitrary", "arbitrary"),
                                             vmem_limit_bytes=VMEM_LIMIT),
        name="mixer_proj",
    )(x2d, mod, g_pre, pos3, invf, wqkv, wf, bfg, e, ec, tri)


DILATIONS = (1, 4, 16)


def _dil_kernel(q_ref, k_ref, v_ref, o_ref, qp, kp, vp, m_sc, l_sc, acc_sc):
    S = q_ref.shape[0]
    nblk = S // BLK
    lane = lax.broadcasted_iota(jnp.int32, (BLK, LANES), 1)
    row = lax.broadcasted_iota(jnp.int32, (BLK, BLK), 0)
    col = lax.broadcasted_iota(jnp.int32, (BLK, BLK), 1)
    cur_ok = col <= row
    prev_ok = col >= row
    first_head = lane < HEAD_DIM

    for p, d in enumerate(DILATIONS):
        n = S // d
        for r in range(d):
            rows = pl.ds(r, n, stride=d) if d > 1 else pl.ds(0, S)
            qp[p, r * n:(r + 1) * n, :] = (q_ref[rows, :] * ATTN_SCALE).astype(BF16)
            kp[p, r * n:(r + 1) * n, :] = k_ref[rows, :].astype(BF16)
            vp[p, r * n:(r + 1) * n, :] = v_ref[rows, :].astype(BF16)

    def block(p, d, ib, has_prev):
        r0 = pl.multiple_of(ib * BLK, BLK)
        q = qp[p, pl.ds(r0, BLK), :]
        kc = kp[p, pl.ds(r0, BLK), :]
        vc = vp[p, pl.ds(r0, BLK), :]
        if has_prev:
            rp = pl.multiple_of(r0 - BLK, BLK)
            kpv = kp[p, pl.ds(rp, BLK), :]
            vpv = vp[p, pl.ds(rp, BLK), :]
        res = []
        for hd in range(2):
            qh = jnp.where(first_head if hd == 0 else jnp.logical_not(first_head), q, jnp.zeros_like(q))
            dn = (((1,), (1,)), ((), ()))
            sc = jnp.where(cur_ok, lax.dot_general(qh, kc, dn, preferred_element_type=F32), NEG)
            m = jnp.max(sc, axis=-1, keepdims=True)
            if has_prev:
                sp = jnp.where(prev_ok, lax.dot_general(qh, kpv, dn, preferred_element_type=F32), NEG)
                m = jnp.maximum(m, jnp.max(sp, axis=-1, keepdims=True))
            ec = jnp.exp(sc - m)
            l = jnp.sum(ec, axis=-1, keepdims=True)
            acc = jnp.dot(ec.astype(BF16), vc, preferred_element_type=F32)
            if has_prev:
                ep = jnp.exp(sp - m)
                l = l + jnp.sum(ep, axis=-1, keepdims=True)
                acc = acc + jnp.dot(ep.astype(BF16), vpv, preferred_element_type=F32)
            res.append((m, l, acc))
        (m0, l0, a0), (m1, l1, a1) = res
        m2 = jnp.where(first_head, m0, m1)
        l2 = jnp.where(first_head, l0, l1)
        a2 = jnp.where(first_head, a0, a1)
        n = S // d
        if d == 1:
            dst = pl.ds(r0, BLK)
        else:
            cls = ib // (n // BLK)
            start = cls + (r0 - cls * n) * d
            dst = pl.ds(start, BLK, stride=d)
        m_sc[p, dst, :] = m2
        l_sc[p, dst, :] = l2
        acc_sc[p, dst, :] = a2

    for p, d in enumerate(DILATIONS):
        per_cls = (S // d) // BLK

        def body(ib, carry, p=p, d=d, per_cls=per_cls):
            if per_cls == 1:
                block(p, d, ib, False)
            else:
                first = (ib % per_cls) == 0

                @pl.when(first)
                def _():
                    block(p, d, ib, False)

                @pl.when(jnp.logical_not(first))
                def _():
                    block(p, d, ib, True)
            return carry

        lax.fori_loop(0, nblk, body, 0)

    def merge(i, carry):
        rows = pl.ds(pl.multiple_of(i * BLK, BLK), BLK)
        ms = [m_sc[p, rows, :] for p in range(3)]
        m = jnp.maximum(jnp.maximum(ms[0], ms[1]), ms[2])
        num = jnp.zeros((BLK, LANES), F32)
        den = jnp.zeros((BLK, LANES), F32)
        for p in range(3):
            w = jnp.exp(ms[p] - m)
            num = num + w * acc_sc[p, rows, :]
            den = den + w * l_sc[p, rows, :]
        o_ref[rows, :] = num / den
        return carry

    lax.fori_loop(0, nblk, merge, 0)


def _dil_call(qa, ka, va, *, batch, seq):
    npair = WIDTH_A // LANES
    spec = pl.BlockSpec((None, seq, LANES), lambda b, j: (b, 0, j))
    q3, k3, v3 = (t.reshape(batch, seq, WIDTH_A) for t in (qa, ka, va))
    return pl.pallas_call(
        _dil_kernel,
        grid=(batch, npair),
        in_specs=[spec, spec, spec],
        out_specs=spec,
        out_shape=jax.ShapeDtypeStruct((batch, seq, WIDTH_A), F32),
        scratch_shapes=[pltpu.VMEM((3, seq, LANES), BF16)] * 3 + [pltpu.VMEM((3, seq, LANES), F32)] * 3,
        compiler_params=pltpu.CompilerParams(dimension_semantics=("arbitrary", "arbitrary"),
                                             vmem_limit_bytes=VMEM_LIMIT),
        name="dilated_attn",
    )(q3, k3, v3)


FOX_T = 256


def _fox_kernel(q_ref, k_ref, v_ref, fcol_ref, frow_ref, o_ref):
    S = q_ref.shape[0]
    nq = S // FOX_T
    lane = lax.broadcasted_iota(jnp.int32, (FOX_T, LANES), 1)
    first_head = lane < HEAD_DIM
    row = lax.broadcasted_iota(jnp.int32, (FOX_T, FOX_T), 0)
    col = lax.broadcasted_iota(jnp.int32, (FOX_T, FOX_T), 1)
    causal = col <= row
    dn = (((1,), (1,)), ((), ()))

    for pr in range(WIDTH_B // LANES):
        lsl = slice(pr * LANES, (pr + 1) * LANES)

        def qblock(qi, carry, pr=pr, lsl=lsl):
            r0 = pl.multiple_of(qi * FOX_T, FOX_T)
            q = q_ref[pl.ds(r0, FOX_T), lsl]
            outs = []
            for hd in range(2):
                h = 2 * pr + hd
                qh = jnp.where(first_head if hd == 0 else jnp.logical_not(first_head), q,
                               jnp.zeros_like(q)) * ATTN_SCALE
                fq = fcol_ref[pl.ds(r0, FOX_T), h:h + 1]

                def step(kj, mla, diag, qh=qh, fq=fq, h=h):
                    m, l, acc = mla
                    c0 = pl.multiple_of(kj * FOX_T, FOX_T)
                    k = k_ref[pl.ds(c0, FOX_T), lsl]
                    v = v_ref[pl.ds(c0, FOX_T), lsl]
                    fk = frow_ref[h, pl.ds(kj, 1), :]
                    s = lax.dot_general(qh, k, dn, preferred_element_type=F32) + fq - fk
                    if diag:
                        s = jnp.where(causal, s, NEG)
                    m_new = jnp.maximum(m, jnp.max(s, axis=-1, keepdims=True))
                    alpha = jnp.exp(m - m_new)
                    pexp = jnp.exp(s - m_new)
                    l = alpha * l + jnp.sum(pexp, axis=-1, keepdims=True)
                    acc = alpha * acc + jnp.dot(pexp.astype(BF16), v, preferred_element_type=F32)
                    return m_new, l, acc

                init = (jnp.full((FOX_T, 1), NEG, F32), jnp.zeros((FOX_T, 1), F32),
                        jnp.zeros((FOX_T, LANES), F32))
                mla = lax.fori_loop(0, qi, functools.partial(step, diag=False), init)
                m, l, acc = step(qi, mla, True)
                outs.append(acc / l)
            o_ref[pl.ds(r0, FOX_T), lsl] = jnp.where(first_head, outs[0], outs[1])
            return carry

        lax.fori_loop(0, nq, qblock, 0)


def _fox_call(qb, kb, vb, fcol, frow, *, batch, seq):
    spec = pl.BlockSpec((None, seq, WIDTH_B), lambda b: (b, 0, 0))
    q3, k3, v3 = (t.reshape(batch, seq, WIDTH_B) for t in (qb, kb, vb))
    nk = seq // FOX_T
    return pl.pallas_call(
        _fox_kernel,
        grid=(batch,),
        in_specs=[spec, spec, spec,
                  pl.BlockSpec((None, seq, LANES), lambda b: (b, 0, 0)),
                  pl.BlockSpec((None, N_HEADS_B, nk, FOX_T), lambda b: (b, 0, 0, 0))],
        out_specs=spec,
        out_shape=jax.ShapeDtypeStruct((batch, seq, WIDTH_B), F32),
        compiler_params=pltpu.CompilerParams(dimension_semantics=("arbitrary",),
                                             vmem_limit_bytes=VMEM_LIMIT),
        name="forget_attn",
    )(q3, k3, v3, fcol.reshape(batch, seq, LANES), frow.reshape(batch, N_HEADS_B, nk, FOX_T))


def _outproj_kernel(oa_ref, ob_ref, x_ref, mod_ref, ga_ref, gb_ref, woa_ref, wob_ref, gpost_ref, o_ref):
    a = _rms(oa_ref[...], ga_ref[...]).astype(BF16)
    b = _rms(ob_ref[...], gb_ref[...]).astype(BF16)
    y = (jnp.dot(a, woa_ref[...], preferred_element_type=F32)
         + jnp.dot(b, wob_ref[...], preferred_element_type=F32))
    gate = mod_ref[0, 5:6, :]
    o_ref[...] = x_ref[...] + gate * _rms(y, gpost_ref[...])


def _outproj_call(oa, ob, x2d, mod, ga, gb, woa, wob, gpost, *, seq, tm):
    T, D = x2d.shape
    per_seq = seq // tm
    return pl.pallas_call(
        _outproj_kernel,
        grid=(T // tm,),
        in_specs=[pl.BlockSpec((tm, WIDTH_A), lambda i: (i, 0)),
                  pl.BlockSpec((tm, WIDTH_B), lambda i: (i, 0)),
                  pl.BlockSpec((tm, D), lambda i: (i, 0)),
                  pl.BlockSpec((1, N_MOD, D), lambda i: (i // per_seq, 0, 0)),
                  _const_spec((1, WIDTH_A)), _const_spec((1, WIDTH_B)),
                  _const_spec(woa.shape), _const_spec(wob.shape), _const_spec((1, D))],
        out_specs=pl.BlockSpec((tm, D), lambda i: (i, 0)),
        out_shape=jax.ShapeDtypeStruct((T, D), F32),
        compiler_params=pltpu.CompilerParams(dimension_semantics=("arbitrary",),
                                             vmem_limit_bytes=VMEM_LIMIT),
        name="mixer_out",
    )(oa, ob, x2d, mod, ga, gb, woa, wob, gpost)


def _ffn_weights(w_gate, w_up, w_down):
    pad = D_FF_PAD - D_FF
    cols = lambda w: jnp.pad(w, ((0, 0), (0, pad))).astype(BF16).reshape(
        D_MODEL, N_FF_CHUNKS, FF_CHUNK).transpose(1, 0, 2)
    wd = jnp.pad(w_down, ((0, pad), (0, 0))).astype(BF16).reshape(N_FF_CHUNKS, FF_CHUNK, D_MODEL)
    return cols(w_gate), cols(w_up), wd


def kernel(x, c, positions, w_ada, b_ada, g_pre_ff1, g_post_ff1, w_ff1_gate, w_ff1_up, w_ff1_down,
           g_pre_mix, g_post_mix, w_in, b_forget, g_out_a, g_out_b, w_out,
           g_pre_ff2, g_post_ff2, w_ff2_gate, w_ff2_up, w_ff2_down):
    B, S, D = x.shape
    depth = w_ada.shape[0]
    tm = 512
    e_np, ec_np = _rope_consts()
    e = jnp.asarray(e_np, BF16)
    ec = jnp.asarray(ec_np, F32)
    tri = jnp.asarray(np.tril(np.ones((tm, tm), np.float32)), BF16)
    inv_freq = ROPE_THETA ** (-jnp.arange(0, ROT_DIM, 2, dtype=F32) / ROT_DIM)
    invf = inv_freq.reshape(ROT_DIM // 2, 1)
    pos3 = positions.reshape(B, 1, S)
    x2d = x.reshape(B * S, D)
    row = lambda g: g.reshape(1, -1)

    for l in range(depth):
        mod = _ada_call(c, w_ada[l], b_ada[l]).reshape(B, N_MOD, D)
        wg, wu, wd = _ffn_weights(w_ff1_gate[l], w_ff1_up[l], w_ff1_down[l])
        x2d = _ffn_call(x2d, mod, row(g_pre_ff1[l]), row(g_post_ff1[l]), wg, wu, wd, row0=0, seq=S, tm=tm)

        wqkv = w_in[l][:, :QKV_COLS].astype(BF16)
        wf = jnp.pad(jnp.tile(w_in[l][:, QKV_COLS:], (1, 3)), ((0, 0), (0, LANES - 3 * N_HEADS_B))).astype(BF16)
        bfg = jnp.pad(jnp.tile(b_forget[l], 3), (0, LANES - 3 * N_HEADS_B)).reshape(1, LANES)
        qa, ka, va, qb, kb, vb, fcol, frow = _proj_call(
            x2d, mod, row(g_pre_mix[l]), pos3, invf, wqkv, wf, bfg, e, ec, tri, batch=B, seq=S, tm=tm)
        oa = _dil_call(qa, ka, va, batch=B, seq=S).reshape(B * S, WIDTH_A)
        ob = _fox_call(qb, kb, vb, fcol, frow, batch=B, seq=S).reshape(B * S, WIDTH_B)
        wo = w_out[l].astype(BF16)
        x2d = _outproj_call(oa, ob, x2d, mod, row(g_out_a[l]), row(g_out_b[l]), wo[:WIDTH_A], wo[WIDTH_A:],
                            row(g_post_mix[l]), seq=S, tm=tm)

        wg, wu, wd = _ffn_weights(w_ff2_gate[l], w_ff2_up[l], w_ff2_down[l])
        x2d = _ffn_call(x2d, mod, row(g_pre_ff2[l]), row(g_post_ff2[l]), wg, wu, wd, row0=6, seq=S, tm=tm)
    return x2d.reshape(B, S, D)
```

```python
import functools
import math

import numpy as np
import jax
import jax.numpy as jnp
from jax import lax
from jax.experimental import pallas as pl
from jax.experimental.pallas import tpu as pltpu

F32 = jnp.float32
BF16 = jnp.bfloat16

D_MODEL = 1024
HEAD_DIM = 64
N_HEADS_A = 8
N_HEADS_B = 8
WIDTH_A = N_HEADS_A * HEAD_DIM
WIDTH_B = N_HEADS_B * HEAD_DIM
ROT_DIM = HEAD_DIM // 4
ROPE_THETA = 500000.0
D_FF = 2752
N_MOD = 9
EPS = 1e-6
ATTN_SCALE = HEAD_DIM ** -0.5
NEG = -1e30
QKV_COLS = 3 * WIDTH_A + 3 * WIDTH_B
LOG2E = math.log2(math.e)
Q_SCALE = ATTN_SCALE * LOG2E

LANES = 128
FF_CHUNK = 256
D_FF_PAD = -(-D_FF // FF_CHUNK) * FF_CHUNK
N_FF_CHUNKS = D_FF_PAD // FF_CHUNK
BLK = 128
AUG = 16
VMEM_LIMIT = 56 * 1024 * 1024


def _rms(x, g):
    ms = jnp.mean(x * x, axis=-1, keepdims=True)
    return x * lax.rsqrt(ms + EPS) * g


def _split3(x):
    hi = x.astype(BF16).astype(F32)
    r1 = x - hi
    mid = r1.astype(BF16).astype(F32)
    lo = (r1 - mid).astype(BF16).astype(F32)
    return hi, mid, lo


def _const_spec(shape):
    nd = len(shape)
    return pl.BlockSpec(shape, lambda *_: (0,) * nd, pipeline_mode=pl.Buffered(1))


def _ada_kernel(c_ref, w_ref, b_ref, o_ref):
    c = c_ref[...]
    sc = (c * jax.nn.sigmoid(c)).astype(BF16)
    o_ref[...] = jnp.dot(sc, w_ref[...].astype(BF16), preferred_element_type=F32) + b_ref[...]


def _ada_call(c, w_ada, b_ada):
    B, D = c.shape
    N = w_ada.shape[1]
    tn = 1024
    return pl.pallas_call(
        _ada_kernel,
        grid=(N // tn,),
        in_specs=[pl.BlockSpec((B, D), lambda j: (0, 0)),
                  pl.BlockSpec((D, tn), lambda j: (0, j)),
                  pl.BlockSpec((1, tn), lambda j: (0, j))],
        out_specs=pl.BlockSpec((B, tn), lambda j: (0, j)),
        out_shape=jax.ShapeDtypeStruct((B, N), F32),
        compiler_params=pltpu.CompilerParams(dimension_semantics=("arbitrary",),
                                             vmem_limit_bytes=VMEM_LIMIT),
        name="adaln",
    )(c, w_ada, b_ada.reshape(1, N))


def _ffn_kernel(x_ref, mod_ref, gpre_ref, gpost_ref, wg_ref, wu_ref, wd_ref, o_ref, acc_ref, *, row0):
    x = x_ref[...]
    shift = mod_ref[0, row0:row0 + 1, :]
    scale = mod_ref[0, row0 + 1:row0 + 2, :]
    gate = mod_ref[0, row0 + 2:row0 + 3, :]
    hb = (_rms(x, gpre_ref[...]) * (1.0 + scale) + shift).astype(BF16)
    acc_ref[...] = jnp.zeros_like(acc_ref)

    def chunk(c, carry):
        g = jnp.dot(hb, wg_ref[c], preferred_element_type=F32)
        u = jnp.dot(hb, wu_ref[c], preferred_element_type=F32)
        a = (g * jax.nn.sigmoid(g) * u).astype(BF16)
        acc_ref[...] += jnp.dot(a, wd_ref[c], preferred_element_type=F32)
        return carry

    lax.fori_loop(0, N_FF_CHUNKS, chunk, 0)
    y = _rms(acc_ref[...], gpost_ref[...])
    o_ref[...] = x + (0.5 * gate) * y


def _ffn_call(x2d, mod, g_pre, g_post, wg, wu, wd, *, row0, seq, tm):
    T, D = x2d.shape
    per_seq = seq // tm
    kern = functools.partial(_ffn_kernel, row0=row0)
    return pl.pallas_call(
        kern,
        grid=(T // tm,),
        in_specs=[pl.BlockSpec((tm, D), lambda i: (i, 0)),
                  pl.BlockSpec((1, N_MOD, D), lambda i: (i // per_seq, 0, 0)),
                  _const_spec((1, D)), _const_spec((1, D)),
                  _const_spec(wg.shape), _const_spec(wu.shape), _const_spec(wd.shape)],
        out_specs=pl.BlockSpec((tm, D), lambda i: (i, 0)),
        out_shape=jax.ShapeDtypeStruct((T, D), F32),
        scratch_shapes=[pltpu.VMEM((tm, D), F32)],
        compiler_params=pltpu.CompilerParams(dimension_semantics=("arbitrary",),
                                             vmem_limit_bytes=VMEM_LIMIT),
        name="ffn_row%d" % row0,
    )(x2d, mod, g_pre, g_post, wg, wu, wd)


def _rope_consts():
    half = ROT_DIM // 2
    e = np.zeros((3, 2 * half, 3 * LANES), np.float32)
    c = np.zeros((1, 3 * LANES), np.float32)
    for lane in range(LANES):
        d = lane % HEAD_DIM
        if d < ROT_DIM:
            e[:, d % half, lane] = 1.0
        else:
            c[0, lane] = 1.0
        if d < half:
            e[:, half + d, LANES + lane] = -1.0
        elif d < ROT_DIM:
            e[:, half + d - half, 2 * LANES + lane] = 1.0
    return e.reshape(3 * 2 * half, 3 * LANES), c


def _forget_consts():
    pq = np.zeros((LANES, LANES), np.float32)
    pk = np.zeros((LANES, LANES), np.float32)
    cq = np.zeros((1, LANES), np.float32)
    ck = np.zeros((1, LANES), np.float32)
    for h in range(N_HEADS_B):
        for g in range(3):
            pq[8 * g + h, AUG * h + g] = 1.0
            cq[0, AUG * h + 3 + g] = 1.0
            pk[8 * g + h, AUG * h + 3 + g] = -1.0
            ck[0, AUG * h + g] = 1.0
    return pq, pk, cq, ck


def _proj_kernel(x_ref, mod_ref, g_ref, pos_ref, invf_ref, wqkv_ref, wf_ref, bf_ref, e_ref, ec_ref, tri_ref,
                 pq_ref, pk_ref, cq_ref, ck_ref,
                 qa_ref, ka_ref, va_ref, qb_ref, kb_ref, vb_ref, aq_ref, ak_ref, carry_ref):
    j = pl.program_id(1)
    tm = x_ref.shape[0]
    x = x_ref[...]
    shift = mod_ref[0, 3:4, :]
    scale = mod_ref[0, 4:5, :]
    hb = (_rms(x, g_ref[...]) * (1.0 + scale) + shift).astype(BF16)
    proj = jnp.dot(hb, wqkv_ref[...], preferred_element_type=F32)

    posf = pos_ref[0].astype(F32)
    ang = invf_ref[...] * posf
    cs = jnp.concatenate([jnp.cos(ang), jnp.sin(ang)], axis=0)
    cs3 = jnp.concatenate([t.astype(BF16) for t in _split3(cs)], axis=0)
    tabs = lax.dot_general(cs3, e_ref[...], (((0,), (0,)), ((), ())),
                           preferred_element_type=F32) + ec_ref[...]
    t_cos = tabs[:, 0:LANES]
    t_s1 = tabs[:, LANES:2 * LANES]
    t_s2 = tabs[:, 2 * LANES:3 * LANES]
    half = ROT_DIM // 2

    def rope(t):
        return t * t_cos + pltpu.roll(t, LANES - half, 1) * t_s1 + pltpu.roll(t, half, 1) * t_s2

    for i in range(WIDTH_A // LANES):
        lo = i * LANES
        qa_ref[:, lo:lo + LANES] = rope(proj[:, lo:lo + LANES]) * Q_SCALE
        ka_ref[:, lo:lo + LANES] = rope(proj[:, WIDTH_A + lo:WIDTH_A + lo + LANES])
    va_ref[...] = proj[:, 2 * WIDTH_A:3 * WIDTH_A]
    o = 3 * WIDTH_A
    qb_ref[...] = (proj[:, o:o + WIDTH_B] * Q_SCALE).astype(BF16)
    kb_ref[...] = proj[:, o + WIDTH_B:o + 2 * WIDTH_B].astype(BF16)
    vb_ref[...] = proj[:, o + 2 * WIDTH_B:o + 3 * WIDTH_B].astype(BF16)

    fl = jnp.dot(hb, wf_ref[...], preferred_element_type=F32) + bf_ref[...]
    lf = jnp.minimum(fl, 0.0) - jnp.log1p(jnp.exp(-jnp.abs(fl)))
    lane = lax.broadcasted_iota(jnp.int32, lf.shape, 1)
    zero = jnp.zeros_like(lf)

    def by_group(a, b, c):
        return jnp.where(lane < 8, a, jnp.where(lane < 16, b, jnp.where(lane < 24, c, zero))).astype(BF16)

    cum = jnp.dot(tri_ref[...], by_group(*_split3(lf)), preferred_element_type=F32)
    cum = cum + pltpu.roll(cum, LANES - 8, 1) + pltpu.roll(cum, LANES - 16, 1)

    @pl.when(j == 0)
    def _():
        carry_ref[...] = jnp.zeros_like(carry_ref)

    f_tot = cum + carry_ref[...]
    carry_ref[...] = f_tot[tm - 1:tm, :]
    p1, p2, p3 = _split3(f_tot * LOG2E)
    pieces = by_group(p1, pltpu.roll(p2, 8, 1), pltpu.roll(p3, 16, 1))
    aq_ref[...] = (jnp.dot(pieces, pq_ref[...], preferred_element_type=F32) + cq_ref[...]).astype(BF16)
    ak_ref[...] = (jnp.dot(pieces, pk_ref[...], preferred_element_type=F32) + ck_ref[...]).astype(BF16)


def _proj_call(x2d, mod, g_pre, pos3, invf, wqkv, wf, bfg, consts, *, batch, seq, tm):
    T, D = x2d.shape
    per_seq = seq // tm
    tok = lambda b, j: (b * per_seq + j, 0)
    out_shape = ([jax.ShapeDtypeStruct((T, WIDTH_A), F32)] * 3
                 + [jax.ShapeDtypeStruct((T, WIDTH_B), BF16)] * 3
                 + [jax.ShapeDtypeStruct((T, LANES), BF16)] * 2)
    out_specs = ([pl.BlockSpec((tm, WIDTH_A), tok)] * 3 + [pl.BlockSpec((tm, WIDTH_B), tok)] * 3
                 + [pl.BlockSpec((tm, LANES), tok)] * 2)
    return pl.pallas_call(
        _proj_kernel,
        grid=(batch, per_seq),
        in_specs=[pl.BlockSpec((tm, D), tok),
                  pl.BlockSpec((1, N_MOD, D), lambda b, j: (b, 0, 0)),
                  _const_spec((1, D)),
                  pl.BlockSpec((None, 1, tm), lambda b, j: (b, 0, j)),
                  _const_spec(invf.shape), _const_spec(wqkv.shape), _const_spec(wf.shape),
                  _const_spec(bfg.shape)] + [_const_spec(t.shape) for t in consts],
        out_specs=out_specs,
        out_shape=out_shape,
        scratch_shapes=[pltpu.VMEM((1, LANES), F32)],
        compiler_params=pltpu.CompilerParams(dimension_semantics=("arbitrary", "arbitrary"),
                                             vmem_limit_bytes=VMEM_LIMIT),
        name="mixer_proj",
    )(x2d, mod, g_pre, pos3, invf, wqkv, wf, bfg, *consts)


DILATIONS = (1, 4, 16)
DIL_AHEAD = 16
PREP_ROWS = 512


def _dil_kernel(q_ref, k_ref, v_ref, o_ref, qp, kp, vp, o_sc, l_sc):
    S = q_ref.shape[0]
    nblk = S // BLK
    first_head = lax.broadcasted_iota(jnp.int32, (BLK, LANES), 1) < HEAD_DIM
    row2 = lax.broadcasted_iota(jnp.int32, (BLK, 2 * BLK), 0)
    col2 = lax.broadcasted_iota(jnp.int32, (BLK, 2 * BLK), 1)
    band = (col2 >= row2) & (col2 <= row2 + BLK)
    row1 = lax.broadcasted_iota(jnp.int32, (BLK, BLK), 0)
    col1 = lax.broadcasted_iota(jnp.int32, (BLK, BLK), 1)
    causal = col1 <= row1
    dn = (((1,), (1,)), ((), ()))

    for p, d in enumerate(DILATIONS):
        n = S // d
        ch = min(n, PREP_ROWS)
        fh = lax.broadcasted_iota(jnp.int32, (ch, LANES), 1) < HEAD_DIM
        one = jnp.ones((ch, LANES), BF16)
        for r in range(d):
            for c0 in range(0, n, ch):
                src = pl.ds(r + c0 * d, ch, stride=d) if d > 1 else pl.ds(c0, ch)
                dst = pl.ds(r * n + c0, ch)
                qp[p, dst, :] = q_ref[src, :].astype(BF16)
                kp[p, dst, :] = k_ref[src, :].astype(BF16)
                v = v_ref[src, :].astype(BF16)
                vp[p, 0, dst, :] = jnp.where(fh, v, one)
                vp[p, 1, dst, :] = jnp.where(fh, one, v)

    tasks = [(p, ib, hd) for p in range(len(DILATIONS)) for ib in range(nblk) for hd in range(2)]

    def keys_of(p, ib):
        per_cls = (S // DILATIONS[p]) // BLK
        if ib % per_cls == 0:
            return pl.ds(ib * BLK, BLK), causal
        return pl.ds((ib - 1) * BLK, 2 * BLK), band

    def scores(p, ib, hd):
        q = qp[p, pl.ds(ib * BLK, BLK), :]
        qh = jnp.where(first_head if hd == 0 else jnp.logical_not(first_head), q, jnp.zeros_like(q))
        keys, mask = keys_of(p, ib)
        s = lax.dot_general(qh, kp[p, keys, :], dn, preferred_element_type=F32)
        s = jnp.where(mask, s, NEG)
        m = jnp.max(s, axis=-1, keepdims=True)
        return m, jnp.exp2(s - m).astype(BF16)

    def finish(p, ib, m0, a0, m1, a1):
        num = jnp.where(first_head, a0, a1)
        den = pltpu.roll(jnp.where(first_head, a1, a0), HEAD_DIM, 1)
        lse = jnp.where(first_head, m0, m1) + jnp.log2(den)
        d = DILATIONS[p]
        n = S // d
        cls, t0 = divmod(ib * BLK, n)
        dst = pl.ds(cls + t0 * d, BLK, stride=d) if d > 1 else pl.ds(t0, BLK)
        o_sc[p, dst, :] = num / den
        l_sc[p, dst, :] = lse

    pending = {}
    for i in range(len(tasks) + DIL_AHEAD):
        if i < len(tasks):
            pending[i] = scores(*tasks[i])
        j = i - DIL_AHEAD
        if j >= 0:
            p, ib, hd = tasks[j]
            m, e = pending.pop(j)
            acc = jnp.dot(e, vp[p, hd, keys_of(p, ib)[0], :], preferred_element_type=F32)
            if hd == 0:
                head0 = (m, acc)
            else:
                finish(p, ib, head0[0], head0[1], m, acc)

    for i in range(nblk):
        rows = pl.ds(i * BLK, BLK)
        ls = [l_sc[p, rows, :] for p in range(3)]
        m = jnp.maximum(jnp.maximum(ls[0], ls[1]), ls[2])
        ws = [jnp.exp2(l - m) for l in ls]
        num = ws[0] * o_sc[0, rows, :] + ws[1] * o_sc[1, rows, :] + ws[2] * o_sc[2, rows, :]
        o_ref[rows, :] = num / (ws[0] + ws[1] + ws[2])


def _dil_call(qa, ka, va, *, batch, seq):
    npair = WIDTH_A // LANES
    spec = pl.BlockSpec((None, seq, LANES), lambda b, j: (b, 0, j))
    q3, k3, v3 = (t.reshape(batch, seq, WIDTH_A) for t in (qa, ka, va))
    return pl.pallas_call(
        _dil_kernel,
        grid=(batch, npair),
        in_specs=[spec, spec, spec],
        out_specs=spec,
        out_shape=jax.ShapeDtypeStruct((batch, seq, WIDTH_A), F32),
        scratch_shapes=[pltpu.VMEM((3, seq, LANES), BF16), pltpu.VMEM((3, seq, LANES), BF16),
                        pltpu.VMEM((3, 2, seq, LANES), BF16),
                        pltpu.VMEM((3, seq, LANES), F32), pltpu.VMEM((3, seq, LANES), F32)],
        compiler_params=pltpu.CompilerParams(dimension_semantics=("arbitrary", "arbitrary"),
                                             vmem_limit_bytes=VMEM_LIMIT),
        name="dilated_attn",
    )(q3, k3, v3)


FOX_T = 256
FOX_AHEAD = 2


def _fox_kernel(q_ref, k_ref, v_ref, aq_ref, ak_ref, o_ref, vaug):
    S = q_ref.shape[0]
    nq = S // FOX_T
    pair = pl.program_id(1)
    lane = lax.broadcasted_iota(jnp.int32, (FOX_T, LANES), 1)
    first_head = lane < HEAD_DIM
    row = lax.broadcasted_iota(jnp.int32, (FOX_T, FOX_T), 0)
    col = lax.broadcasted_iota(jnp.int32, (FOX_T, FOX_T), 1)
    causal = col <= row
    dn = (((1,), (1,)), ((), ()))

    fh = lax.broadcasted_iota(jnp.int32, (PREP_ROWS, LANES), 1) < HEAD_DIM
    one = jnp.ones((PREP_ROWS, LANES), BF16)
    for c0 in range(0, S, PREP_ROWS):
        v = v_ref[c0:c0 + PREP_ROWS, :]
        vaug[0, c0:c0 + PREP_ROWS, :] = jnp.where(fh, v, one)
        vaug[1, c0:c0 + PREP_ROWS, :] = jnp.where(fh, one, v)

    tasks = [(qi, hd) for qi in range(nq) for hd in range(2)]

    def scores(qi, hd):
        r0, n = qi * FOX_T, (qi + 1) * FOX_T
        q = q_ref[r0:r0 + FOX_T, :]
        aq = aq_ref[r0:r0 + FOX_T, :]
        lo = AUG * (2 * pair + hd)
        own = (lane >= lo) & (lane < lo + AUG)
        ql = jnp.concatenate(
            [jnp.where(first_head if hd == 0 else jnp.logical_not(first_head), q, jnp.zeros_like(q)),
             jnp.where(own, aq, jnp.zeros_like(aq))], axis=1)
        kk = jnp.concatenate([k_ref[0:n, :], ak_ref[0:n, :]], axis=1)
        s = lax.dot_general(ql, kk, dn, preferred_element_type=F32)
        tiles = [s[:, c:c + LANES] for c in range(0, n - FOX_T, LANES)]
        diag = jnp.where(causal, s[:, n - FOX_T:n], NEG)
        tiles += [diag[:, c:c + LANES] for c in range(0, FOX_T, LANES)]
        mx = tiles[0]
        for t in tiles[1:]:
            mx = jnp.maximum(mx, t)
        m = jnp.max(mx, axis=-1, keepdims=True)
        return jnp.concatenate([jnp.exp2(t - m).astype(BF16) for t in tiles], axis=1)

    pending = {}
    for i in range(len(tasks) + FOX_AHEAD):
        if i < len(tasks):
            pending[i] = scores(*tasks[i])
        j = i - FOX_AHEAD
        if j >= 0:
            qi, hd = tasks[j]
            n = (qi + 1) * FOX_T
            acc = jnp.dot(pending.pop(j), vaug[hd, 0:n, :], preferred_element_type=F32)
            if hd == 0:
                a0 = acc
            else:
                num = jnp.where(first_head, a0, acc)
                den = pltpu.roll(jnp.where(first_head, acc, a0), HEAD_DIM, 1)
                o_ref[qi * FOX_T:(qi + 1) * FOX_T, :] = num / den


def _fox_call(qb, kb, vb, aq, ak, *, batch, seq):
    npair = WIDTH_B // LANES
    spec = pl.BlockSpec((None, seq, LANES), lambda b, j: (b, 0, j))
    aspec = pl.BlockSpec((None, seq, LANES), lambda b, j: (b, 0, 0))
    q3, k3, v3 = (t.reshape(batch, seq, WIDTH_B) for t in (qb, kb, vb))
    return pl.pallas_call(
        _fox_kernel,
        grid=(batch, npair),
        in_specs=[spec, spec, spec, aspec, aspec],
        out_specs=spec,
        out_shape=jax.ShapeDtypeStruct((batch, seq, WIDTH_B), F32),
        scratch_shapes=[pltpu.VMEM((2, seq, LANES), BF16)],
        compiler_params=pltpu.CompilerParams(dimension_semantics=("arbitrary", "arbitrary"),
                                             vmem_limit_bytes=VMEM_LIMIT),
        name="forget_attn",
    )(q3, k3, v3, aq.reshape(batch, seq, LANES), ak.reshape(batch, seq, LANES))


def _outproj_kernel(oa_ref, ob_ref, x_ref, mod_ref, ga_ref, gb_ref, woa_ref, wob_ref, gpost_ref, o_ref):
    a = _rms(oa_ref[...], ga_ref[...]).astype(BF16)
    b = _rms(ob_ref[...], gb_ref[...]).astype(BF16)
    y = (jnp.dot(a, woa_ref[...], preferred_element_type=F32)
         + jnp.dot(b, wob_ref[...], preferred_element_type=F32))
    gate = mod_ref[0, 5:6, :]
    o_ref[...] = x_ref[...] + gate * _rms(y, gpost_ref[...])


def _outproj_call(oa, ob, x2d, mod, ga, gb, woa, wob, gpost, *, seq, tm):
    T, D = x2d.shape
    per_seq = seq // tm
    return pl.pallas_call(
        _outproj_kernel,
        grid=(T // tm,),
        in_specs=[pl.BlockSpec((tm, WIDTH_A), lambda i: (i, 0)),
                  pl.BlockSpec((tm, WIDTH_B), lambda i: (i, 0)),
                  pl.BlockSpec((tm, D), lambda i: (i, 0)),
                  pl.BlockSpec((1, N_MOD, D), lambda i: (i // per_seq, 0, 0)),
                  _const_spec((1, WIDTH_A)), _const_spec((1, WIDTH_B)),
                  _const_spec(woa.shape), _const_spec(wob.shape), _const_spec((1, D))],
        out_specs=pl.BlockSpec((tm, D), lambda i: (i, 0)),
        out_shape=jax.ShapeDtypeStruct((T, D), F32),
        compiler_params=pltpu.CompilerParams(dimension_semantics=("arbitrary",),
                                             vmem_limit_bytes=VMEM_LIMIT),
        name="mixer_out",
    )(oa, ob, x2d, mod, ga, gb, woa, wob, gpost)


def _ffn_weights(w_gate, w_up, w_down):
    pad = D_FF_PAD - D_FF
    cols = lambda w: jnp.pad(w, ((0, 0), (0, pad))).astype(BF16).reshape(
        D_MODEL, N_FF_CHUNKS, FF_CHUNK).transpose(1, 0, 2)
    wd = jnp.pad(w_down, ((0, pad), (0, 0))).astype(BF16).reshape(N_FF_CHUNKS, FF_CHUNK, D_MODEL)
    return cols(w_gate), cols(w_up), wd


def kernel(x, c, positions, w_ada, b_ada, g_pre_ff1, g_post_ff1, w_ff1_gate, w_ff1_up, w_ff1_down,
           g_pre_mix, g_post_mix, w_in, b_forget, g_out_a, g_out_b, w_out,
           g_pre_ff2, g_post_ff2, w_ff2_gate, w_ff2_up, w_ff2_down):
    B, S, D = x.shape
    depth = w_ada.shape[0]
    tm = 512
    e_np, ec_np = _rope_consts()
    pq, pk, cq, ck = _forget_consts()
    tri = np.tril(np.ones((tm, tm), np.float32))
    consts = (jnp.asarray(e_np, BF16), jnp.asarray(ec_np, F32), jnp.asarray(tri, BF16),
              jnp.asarray(pq, BF16), jnp.asarray(pk, BF16), jnp.asarray(cq, F32), jnp.asarray(ck, F32))
    inv_freq = ROPE_THETA ** (-jnp.arange(0, ROT_DIM, 2, dtype=F32) / ROT_DIM)
    invf = inv_freq.reshape(ROT_DIM // 2, 1)
    pos3 = positions.reshape(B, 1, S)
    x2d = x.reshape(B * S, D)
    row = lambda g: g.reshape(1, -1)

    for l in range(depth):
        mod = _ada_call(c, w_ada[l], b_ada[l]).reshape(B, N_MOD, D)
        wg, wu, wd = _ffn_weights(w_ff1_gate[l], w_ff1_up[l], w_ff1_down[l])
        x2d = _ffn_call(x2d, mod, row(g_pre_ff1[l]), row(g_post_ff1[l]), wg, wu, wd, row0=0, seq=S, tm=tm)

        wqkv = w_in[l][:, :QKV_COLS].astype(BF16)
        wf = jnp.pad(jnp.tile(w_in[l][:, QKV_COLS:], (1, 3)), ((0, 0), (0, LANES - 3 * N_HEADS_B))).astype(BF16)
        bfg = jnp.pad(jnp.tile(b_forget[l], 3), (0, LANES - 3 * N_HEADS_B)).reshape(1, LANES)
        qa, ka, va, qb, kb, vb, aq, ak = _proj_call(
            x2d, mod, row(g_pre_mix[l]), pos3, invf, wqkv, wf, bfg, consts, batch=B, seq=S, tm=tm)
        oa = _dil_call(qa, ka, va, batch=B, seq=S).reshape(B * S, WIDTH_A)
        ob = _fox_call(qb, kb, vb, aq, ak, batch=B, seq=S).reshape(B * S, WIDTH_B)
        wo = w_out[l].astype(BF16)
        x2d = _outproj_call(oa, ob, x2d, mod, row(g_out_a[l]), row(g_out_b[l]), wo[:WIDTH_A], wo[WIDTH_A:],
                            row(g_post_mix[l]), seq=S, tm=tm)

        wg, wu, wd = _ffn_weights(w_ff2_gate[l], w_ff2_up[l], w_ff2_down[l])
        x2d = _ffn_call(x2d, mod, row(g_pre_ff2[l]), row(g_post_ff2[l]), wg, wu, wd, row0=6, seq=S, tm=tm)
    return x2d.reshape(B, S, D)
```

```python
import functools
import math

import numpy as np
import jax
import jax.numpy as jnp
from jax import lax
from jax.experimental import pallas as pl
from jax.experimental.pallas import tpu as pltpu

F32 = jnp.float32
BF16 = jnp.bfloat16

D_MODEL = 1024
HEAD_DIM = 64
N_HEADS_A = 8
N_HEADS_B = 8
WIDTH_A = N_HEADS_A * HEAD_DIM
WIDTH_B = N_HEADS_B * HEAD_DIM
ROT_DIM = HEAD_DIM // 4
ROPE_THETA = 500000.0
D_FF = 2752
N_MOD = 9
EPS = 1e-6
ATTN_SCALE = HEAD_DIM ** -0.5
NEG = -1e30
QKV_COLS = 3 * WIDTH_A + 3 * WIDTH_B
LOG2E = math.log2(math.e)
Q_SCALE = ATTN_SCALE * LOG2E

LANES = 128
FF_CHUNK = 256
D_FF_PAD = -(-D_FF // FF_CHUNK) * FF_CHUNK
N_FF_CHUNKS = D_FF_PAD // FF_CHUNK
BLK = 128
AUG = 16
VMEM_LIMIT = 56 * 1024 * 1024


def _rms(x, g):
    ms = jnp.mean(x * x, axis=-1, keepdims=True)
    return x * lax.rsqrt(ms + EPS) * g


def _split3(x):
    hi = x.astype(BF16).astype(F32)
    r1 = x - hi
    mid = r1.astype(BF16).astype(F32)
    lo = (r1 - mid).astype(BF16).astype(F32)
    return hi, mid, lo


def _const_spec(shape):
    nd = len(shape)
    return pl.BlockSpec(shape, lambda *_: (0,) * nd, pipeline_mode=pl.Buffered(1))


def _ada_kernel(c_ref, w_ref, b_ref, o_ref):
    c = c_ref[...]
    sc = (c * jax.nn.sigmoid(c)).astype(BF16)
    o_ref[...] = jnp.dot(sc, w_ref[...].astype(BF16), preferred_element_type=F32) + b_ref[...]


def _ada_call(c, w_ada, b_ada):
    B, D = c.shape
    N = w_ada.shape[1]
    tn = 1024
    return pl.pallas_call(
        _ada_kernel,
        grid=(N // tn,),
        in_specs=[pl.BlockSpec((B, D), lambda j: (0, 0)),
                  pl.BlockSpec((D, tn), lambda j: (0, j)),
                  pl.BlockSpec((1, tn), lambda j: (0, j))],
        out_specs=pl.BlockSpec((B, tn), lambda j: (0, j)),
        out_shape=jax.ShapeDtypeStruct((B, N), F32),
        compiler_params=pltpu.CompilerParams(dimension_semantics=("arbitrary",),
                                             vmem_limit_bytes=VMEM_LIMIT),
        name="adaln",
    )(c, w_ada, b_ada.reshape(1, N))


def _ffn_kernel(x_ref, mod_ref, gpre_ref, gpost_ref, wg_ref, wu_ref, wd_ref, o_ref, a_ref, *, row0):
    x = x_ref[...]
    shift = mod_ref[0, row0:row0 + 1, :]
    gs = gpre_ref[...] * (1.0 + mod_ref[0, row0 + 1:row0 + 2, :])
    gp = gpost_ref[...] * (0.5 * mod_ref[0, row0 + 2:row0 + 3, :])
    ms = jnp.mean(x * x, axis=-1, keepdims=True)
    hb = (x * lax.rsqrt(ms + EPS) * gs + shift).astype(BF16)
    for c in range(N_FF_CHUNKS):
        g = jnp.dot(hb, wg_ref[c], preferred_element_type=F32)
        u = jnp.dot(hb, wu_ref[c], preferred_element_type=F32)
        a_ref[:, c * FF_CHUNK:(c + 1) * FF_CHUNK] = (g * jax.nn.sigmoid(g) * u).astype(BF16)
    y = jnp.dot(a_ref[...], wd_ref[...], preferred_element_type=F32)
    o_ref[...] = x + _rms(y, gp)


def _ffn_call(x2d, mod, g_pre, g_post, wg, wu, wd, *, row0, seq, tm):
    T, D = x2d.shape
    per_seq = seq // tm
    kern = functools.partial(_ffn_kernel, row0=row0)
    return pl.pallas_call(
        kern,
        grid=(T // tm,),
        in_specs=[pl.BlockSpec((tm, D), lambda i: (i, 0)),
                  pl.BlockSpec((1, N_MOD, D), lambda i: (i // per_seq, 0, 0)),
                  _const_spec((1, D)), _const_spec((1, D)),
                  _const_spec(wg.shape), _const_spec(wu.shape), _const_spec(wd.shape)],
        out_specs=pl.BlockSpec((tm, D), lambda i: (i, 0)),
        out_shape=jax.ShapeDtypeStruct((T, D), F32),
        scratch_shapes=[pltpu.VMEM((tm, D_FF_PAD), BF16)],
        compiler_params=pltpu.CompilerParams(dimension_semantics=("arbitrary",),
                                             vmem_limit_bytes=VMEM_LIMIT),
        name="ffn_row%d" % row0,
    )(x2d, mod, g_pre, g_post, wg, wu, wd)


def _rope_consts():
    half = ROT_DIM // 2
    e = np.zeros((3, 2 * half, 3 * LANES), np.float32)
    c = np.zeros((1, 3 * LANES), np.float32)
    for lane in range(LANES):
        d = lane % HEAD_DIM
        if d < ROT_DIM:
            e[:, d % half, lane] = 1.0
        else:
            c[0, lane] = 1.0
        if d < half:
            e[:, half + d, LANES + lane] = -1.0
        elif d < ROT_DIM:
            e[:, half + d - half, 2 * LANES + lane] = 1.0
    return e.reshape(3 * 2 * half, 3 * LANES), c


def _forget_consts():
    pq = np.zeros((LANES, LANES), np.float32)
    pk = np.zeros((LANES, LANES), np.float32)
    cq = np.zeros((1, LANES), np.float32)
    ck = np.zeros((1, LANES), np.float32)
    for h in range(N_HEADS_B):
        for g in range(3):
            pq[8 * g + h, AUG * h + g] = 1.0
            cq[0, AUG * h + 3 + g] = 1.0
            pk[8 * g + h, AUG * h + 3 + g] = -1.0
            ck[0, AUG * h + g] = 1.0
    return pq, pk, cq, ck


def _proj_kernel(x_ref, mod_ref, g_ref, pos_ref, invf_ref, wqkv_ref, wf_ref, bf_ref, e_ref, ec_ref, tri_ref,
                 pq_ref, pk_ref, cq_ref, ck_ref,
                 qa_ref, ka_ref, va_ref, qb_ref, kb_ref, vb_ref, aq_ref, ak_ref, carry_ref):
    tm = x_ref.shape[0]

    @pl.when(pl.program_id(1) == 0)
    def _():
        carry_ref[...] = jnp.zeros_like(carry_ref)

    x = x_ref[...]
    shift = mod_ref[0, 3:4, :]
    gs = g_ref[...] * (1.0 + mod_ref[0, 4:5, :])
    ms = jnp.mean(x * x, axis=-1, keepdims=True)
    hb = (x * lax.rsqrt(ms + EPS) * gs + shift).astype(BF16)

    def cols(c0, width):
        return jnp.dot(hb, wqkv_ref[:, c0:c0 + width], preferred_element_type=F32)


    fl = jnp.dot(hb, wf_ref[...], preferred_element_type=F32) + bf_ref[...]
    va_ref[...] = cols(2 * WIDTH_A, WIDTH_A)

    posf = pos_ref[0].astype(F32)
    ang = invf_ref[...] * posf
    cs = jnp.concatenate([jnp.cos(ang), jnp.sin(ang)], axis=0)
    cs3 = jnp.concatenate([t.astype(BF16) for t in _split3(cs)], axis=0)
    tabs = lax.dot_general(cs3, e_ref[...], (((0,), (0,)), ((), ())),
                           preferred_element_type=F32) + ec_ref[...]
    t_cos = tabs[:, 0:LANES]
    t_s1 = tabs[:, LANES:2 * LANES]
    t_s2 = tabs[:, 2 * LANES:3 * LANES]
    half = ROT_DIM // 2

    def rope(t):
        return t * t_cos + pltpu.roll(t, LANES - half, 1) * t_s1 + pltpu.roll(t, half, 1) * t_s2

    lf = jnp.minimum(fl, 0.0) - jnp.log1p(jnp.exp(-jnp.abs(fl)))
    lane = lax.broadcasted_iota(jnp.int32, lf.shape, 1)
    zero = jnp.zeros_like(lf)

    def by_group(a, b, c):
        return jnp.where(lane < 8, a, jnp.where(lane < 16, b, jnp.where(lane < 24, c, zero))).astype(BF16)

    cum = jnp.dot(tri_ref[...], by_group(*_split3(lf)), preferred_element_type=F32)

    qa = cols(0, WIDTH_A)
    for i in range(WIDTH_A // LANES):
        qa_ref[:, i * LANES:(i + 1) * LANES] = rope(qa[:, i * LANES:(i + 1) * LANES]) * Q_SCALE

    cum = cum + pltpu.roll(cum, LANES - 8, 1) + pltpu.roll(cum, LANES - 16, 1)
    f_tot = cum + carry_ref[...]
    carry_ref[...] = f_tot[tm - 1:tm, :]
    p1, p2, p3 = _split3(f_tot * LOG2E)
    pieces = by_group(p1, pltpu.roll(p2, 8, 1), pltpu.roll(p3, 16, 1))
    aq_ref[...] = (jnp.dot(pieces, pq_ref[...], preferred_element_type=F32) + cq_ref[...]).astype(BF16)
    ak_ref[...] = (jnp.dot(pieces, pk_ref[...], preferred_element_type=F32) + ck_ref[...]).astype(BF16)

    ka = cols(WIDTH_A, WIDTH_A)
    for i in range(WIDTH_A // LANES):
        ka_ref[:, i * LANES:(i + 1) * LANES] = rope(ka[:, i * LANES:(i + 1) * LANES])
    o = 3 * WIDTH_A
    qb_ref[...] = (cols(o, WIDTH_B) * Q_SCALE).astype(BF16)
    kb_ref[...] = cols(o + WIDTH_B, WIDTH_B).astype(BF16)
    vb_ref[...] = cols(o + 2 * WIDTH_B, WIDTH_B).astype(BF16)


def _proj_call(x2d, mod, g_pre, pos3, invf, wqkv, wf, bfg, consts, *, batch, seq, tm):
    T, D = x2d.shape
    per_seq = seq // tm
    tok = lambda b, j: (b * per_seq + j, 0)
    out_shape = ([jax.ShapeDtypeStruct((T, WIDTH_A), F32)] * 3
                 + [jax.ShapeDtypeStruct((T, WIDTH_B), BF16)] * 3
                 + [jax.ShapeDtypeStruct((T, LANES), BF16)] * 2)
    out_specs = ([pl.BlockSpec((tm, WIDTH_A), tok)] * 3 + [pl.BlockSpec((tm, WIDTH_B), tok)] * 3
                 + [pl.BlockSpec((tm, LANES), tok)] * 2)
    return pl.pallas_call(
        _proj_kernel,
        grid=(batch, per_seq),
        in_specs=[pl.BlockSpec((tm, D), tok),
                  pl.BlockSpec((1, N_MOD, D), lambda b, j: (b, 0, 0)),
                  _const_spec((1, D)),
                  pl.BlockSpec((None, 1, tm), lambda b, j: (b, 0, j)),
                  _const_spec(invf.shape), _const_spec(wqkv.shape), _const_spec(wf.shape),
                  _const_spec(bfg.shape)] + [_const_spec(t.shape) for t in consts],
        out_specs=out_specs,
        out_shape=out_shape,
        scratch_shapes=[pltpu.VMEM((1, LANES), F32)],
        compiler_params=pltpu.CompilerParams(dimension_semantics=("arbitrary", "arbitrary"),
                                             vmem_limit_bytes=VMEM_LIMIT),
        name="mixer_proj",
    )(x2d, mod, g_pre, pos3, invf, wqkv, wf, bfg, *consts)


DILATIONS = (1, 4, 16)
DIL_AHEAD = 8
PREP_ROWS = 512


def _dil_consts():
    a = np.arange(BLK)[:, None]
    j = np.arange(2 * BLK)[None, :]
    band_ok = np.where(j < BLK, j >= a, j - BLK <= a)
    causal_ok = np.arange(BLK)[None, :] <= a
    bias = lambda ok: np.where(ok, 0.0, NEG).astype(np.float32)
    return bias(band_ok), bias(causal_ok)


def _dil_kernel(q_ref, k_ref, v_ref, band_ref, causal_ref, o_ref, qp, kp, vp, f4, o_sc, l_sc):
    S = q_ref.shape[0]
    nblk = S // BLK
    first_head = lax.broadcasted_iota(jnp.int32, (BLK, LANES), 1) < HEAD_DIM
    dn = (((1,), (1,)), ((), ()))

    def put(p, a, dst, rows):
        rb = rows.astype(BF16)
        if a == 0:
            qp[p, dst, :] = rb
        elif a == 1:
            kp[p, dst, :] = rb
        else:
            fh = lax.broadcasted_iota(jnp.int32, rb.shape, 1) < HEAD_DIM
            one = jnp.ones_like(rb)
            vp[p, 0, dst, :] = jnp.where(fh, rb, one)
            vp[p, 1, dst, :] = jnp.where(fh, one, rb)

    n4, n16 = S // 4, S // 16
    for a, ref in enumerate((q_ref, k_ref, v_ref)):
        for c0 in range(0, S, PREP_ROWS):
            put(0, a, pl.ds(c0, PREP_ROWS), ref[pl.ds(c0, PREP_ROWS), :])
        for r in range(4):
            rows = ref[pl.ds(r, n4, stride=4), :]
            f4[a, pl.ds(r * n4, n4), :] = rows
            put(1, a, pl.ds(r * n4, n4), rows)
        for r4 in range(4):
            for jj in range(4):
                rows = f4[a, pl.ds(r4 * n4 + jj, n16, stride=4), :]
                put(2, a, pl.ds((4 * jj + r4) * n16, n16), rows)

    tasks = [(p, ib) for p in (2, 1, 0) for ib in range(nblk)]

    def keys_of(p, ib):
        per_cls = (S // DILATIONS[p]) // BLK
        if ib % per_cls == 0:
            return pl.ds(ib * BLK, BLK), causal_ref
        return pl.ds((ib - 1) * BLK, 2 * BLK), band_ref

    def scores(p, ib):
        q = qp[p, pl.ds(ib * BLK, BLK), :]
        zero = jnp.zeros_like(q)
        q2 = jnp.concatenate([jnp.where(first_head, q, zero), jnp.where(first_head, zero, q)], axis=0)
        keys, bias_ref = keys_of(p, ib)
        s = lax.dot_general(q2, kp[p, keys, :], dn, preferred_element_type=F32)
        bias = bias_ref[...]
        out = []
        for hd in range(2):
            sh = s[hd * BLK:(hd + 1) * BLK, :] + bias
            m = jnp.max(sh, axis=-1, keepdims=True)
            out.append((m, jnp.exp2(sh - m).astype(BF16)))
        return out

    def finish(p, ib, m0, a0, m1, a1):
        num = jnp.where(first_head, a0, a1)
        den = pltpu.roll(jnp.where(first_head, a1, a0), HEAD_DIM, 1)
        o = num / den
        lse = jnp.where(first_head, m0, m1) + jnp.log2(den)
        d = DILATIONS[p]
        if d > 1:
            cls, t0 = divmod(ib * BLK, S // d)
            dst = pl.ds(cls + t0 * d, BLK, stride=d)
            o_sc[p, dst, :] = o
            l_sc[p, dst, :] = lse
            return
        rows = pl.ds(ib * BLK, BLK)
        ls = [lse, l_sc[1, rows, :], l_sc[2, rows, :]]
        mx = jnp.maximum(jnp.maximum(ls[0], ls[1]), ls[2])
        ws = [jnp.exp2(l - mx) for l in ls]
        tot = ws[0] * o + ws[1] * o_sc[1, rows, :] + ws[2] * o_sc[2, rows, :]
        o_ref[rows, :] = tot / (ws[0] + ws[1] + ws[2])

    pending = {}
    for i in range(len(tasks) + DIL_AHEAD):
        if i < len(tasks):
            pending[i] = scores(*tasks[i])
        j = i - DIL_AHEAD
        if j >= 0:
            p, ib = tasks[j]
            (m0, e0), (m1, e1) = pending.pop(j)
            keys = keys_of(p, ib)[0]
            a0 = jnp.dot(e0, vp[p, 0, keys, :], preferred_element_type=F32)
            a1 = jnp.dot(e1, vp[p, 1, keys, :], preferred_element_type=F32)
            finish(p, ib, m0, a0, m1, a1)


def _dil_call(qa, ka, va, *, batch, seq):
    npair = WIDTH_A // LANES
    spec = pl.BlockSpec((None, seq, LANES), lambda b, j: (b, 0, j))
    q3, k3, v3 = (t.reshape(batch, seq, WIDTH_A) for t in (qa, ka, va))
    consts = [jnp.asarray(t, F32) for t in _dil_consts()]
    return pl.pallas_call(
        _dil_kernel,
        grid=(batch, npair),
        in_specs=[spec, spec, spec] + [_const_spec(t.shape) for t in consts],
        out_specs=spec,
        out_shape=jax.ShapeDtypeStruct((batch, seq, WIDTH_A), F32),
        scratch_shapes=[pltpu.VMEM((3, seq, LANES), BF16), pltpu.VMEM((3, seq, LANES), BF16),
                        pltpu.VMEM((3, 2, seq, LANES), BF16), pltpu.VMEM((3, seq, LANES), F32),
                        pltpu.VMEM((3, seq, LANES), F32), pltpu.VMEM((3, seq, LANES), F32)],
        compiler_params=pltpu.CompilerParams(dimension_semantics=("arbitrary", "arbitrary"),
                                             vmem_limit_bytes=VMEM_LIMIT),
        name="dilated_attn",
    )(q3, k3, v3, *consts)


FOX_T = 256
FOX_AHEAD = 2


def _fox_kernel(q_ref, k_ref, v_ref, aq_ref, ak_ref, o_ref, vaug):
    S = q_ref.shape[0]
    nq = S // FOX_T
    pair = pl.program_id(1)
    lane = lax.broadcasted_iota(jnp.int32, (FOX_T, LANES), 1)
    first_head = lane < HEAD_DIM
    row = lax.broadcasted_iota(jnp.int32, (FOX_T, FOX_T), 0)
    col = lax.broadcasted_iota(jnp.int32, (FOX_T, FOX_T), 1)
    causal = col <= row
    dn = (((1,), (1,)), ((), ()))

    fh = lax.broadcasted_iota(jnp.int32, (PREP_ROWS, LANES), 1) < HEAD_DIM
    one = jnp.ones((PREP_ROWS, LANES), BF16)
    for c0 in range(0, S, PREP_ROWS):
        v = v_ref[c0:c0 + PREP_ROWS, :]
        vaug[0, c0:c0 + PREP_ROWS, :] = jnp.where(fh, v, one)
        vaug[1, c0:c0 + PREP_ROWS, :] = jnp.where(fh, one, v)

    tasks = [(qi, hd) for qi in range(nq) for hd in range(2)]

    def scores(qi, hd):
        r0, n = qi * FOX_T, (qi + 1) * FOX_T
        q = q_ref[r0:r0 + FOX_T, :]
        aq = aq_ref[r0:r0 + FOX_T, :]
        lo = AUG * (2 * pair + hd)
        own = (lane >= lo) & (lane < lo + AUG)
        ql = jnp.concatenate(
            [jnp.where(first_head if hd == 0 else jnp.logical_not(first_head), q, jnp.zeros_like(q)),
             jnp.where(own, aq, jnp.zeros_like(aq))], axis=1)
        kk = jnp.concatenate([k_ref[0:n, :], ak_ref[0:n, :]], axis=1)
        s = lax.dot_general(ql, kk, dn, preferred_element_type=F32)
        tiles = [s[:, c:c + LANES] for c in range(0, n - FOX_T, LANES)]
        diag = jnp.where(causal, s[:, n - FOX_T:n], NEG)
        tiles += [diag[:, c:c + LANES] for c in range(0, FOX_T, LANES)]
        mx = tiles[0]
        for t in tiles[1:]:
            mx = jnp.maximum(mx, t)
        m = jnp.max(mx, axis=-1, keepdims=True)
        return jnp.concatenate([jnp.exp2(t - m).astype(BF16) for t in tiles], axis=1)

    pending = {}
    for i in range(len(tasks) + FOX_AHEAD):
        if i < len(tasks):
            pending[i] = scores(*tasks[i])
        j = i - FOX_AHEAD
        if j >= 0:
            qi, hd = tasks[j]
            n = (qi + 1) * FOX_T
            acc = jnp.dot(pending.pop(j), vaug[hd, 0:n, :], preferred_element_type=F32)
            if hd == 0:
                a0 = acc
            else:
                num = jnp.where(first_head, a0, acc)
                den = pltpu.roll(jnp.where(first_head, acc, a0), HEAD_DIM, 1)
                o_ref[qi * FOX_T:(qi + 1) * FOX_T, :] = num / den


def _fox_call(qb, kb, vb, aq, ak, *, batch, seq):
    npair = WIDTH_B // LANES
    spec = pl.BlockSpec((None, seq, LANES), lambda b, j: (b, 0, j))
    aspec = pl.BlockSpec((None, seq, LANES), lambda b, j: (b, 0, 0))
    q3, k3, v3 = (t.reshape(batch, seq, WIDTH_B) for t in (qb, kb, vb))
    return pl.pallas_call(
        _fox_kernel,
        grid=(batch, npair),
        in_specs=[spec, spec, spec, aspec, aspec],
        out_specs=spec,
        out_shape=jax.ShapeDtypeStruct((batch, seq, WIDTH_B), F32),
        scratch_shapes=[pltpu.VMEM((2, seq, LANES), BF16)],
        compiler_params=pltpu.CompilerParams(dimension_semantics=("arbitrary", "arbitrary"),
                                             vmem_limit_bytes=VMEM_LIMIT),
        name="forget_attn",
    )(q3, k3, v3, aq.reshape(batch, seq, LANES), ak.reshape(batch, seq, LANES))


def _outproj_kernel(oa_ref, ob_ref, x_ref, mod_ref, ga_ref, gb_ref, woa_ref, wob_ref, gpost_ref, o_ref):
    a = _rms(oa_ref[...], ga_ref[...]).astype(BF16)
    b = _rms(ob_ref[...], gb_ref[...]).astype(BF16)
    y = (jnp.dot(a, woa_ref[...], preferred_element_type=F32)
         + jnp.dot(b, wob_ref[...], preferred_element_type=F32))
    gate = mod_ref[0, 5:6, :]
    o_ref[...] = x_ref[...] + gate * _rms(y, gpost_ref[...])


def _outproj_call(oa, ob, x2d, mod, ga, gb, woa, wob, gpost, *, seq, tm):
    T, D = x2d.shape
    per_seq = seq // tm
    return pl.pallas_call(
        _outproj_kernel,
        grid=(T // tm,),
        in_specs=[pl.BlockSpec((tm, WIDTH_A), lambda i: (i, 0)),
                  pl.BlockSpec((tm, WIDTH_B), lambda i: (i, 0)),
                  pl.BlockSpec((tm, D), lambda i: (i, 0)),
                  pl.BlockSpec((1, N_MOD, D), lambda i: (i // per_seq, 0, 0)),
                  _const_spec((1, WIDTH_A)), _const_spec((1, WIDTH_B)),
                  _const_spec(woa.shape), _const_spec(wob.shape), _const_spec((1, D))],
        out_specs=pl.BlockSpec((tm, D), lambda i: (i, 0)),
        out_shape=jax.ShapeDtypeStruct((T, D), F32),
        compiler_params=pltpu.CompilerParams(dimension_semantics=("arbitrary",),
                                             vmem_limit_bytes=VMEM_LIMIT),
        name="mixer_out",
    )(oa, ob, x2d, mod, ga, gb, woa, wob, gpost)


def _ffn_weights(w_gate, w_up, w_down):
    pad = D_FF_PAD - D_FF
    cols = lambda w: jnp.pad(w, ((0, 0), (0, pad))).astype(BF16).reshape(
        D_MODEL, N_FF_CHUNKS, FF_CHUNK).transpose(1, 0, 2)
    wd = jnp.pad(w_down, ((0, pad), (0, 0))).astype(BF16)
    return cols(w_gate), cols(w_up), wd


def kernel(x, c, positions, w_ada, b_ada, g_pre_ff1, g_post_ff1, w_ff1_gate, w_ff1_up, w_ff1_down,
           g_pre_mix, g_post_mix, w_in, b_forget, g_out_a, g_out_b, w_out,
           g_pre_ff2, g_post_ff2, w_ff2_gate, w_ff2_up, w_ff2_down):
    B, S, D = x.shape
    depth = w_ada.shape[0]
    tm = 512
    e_np, ec_np = _rope_consts()
    pq, pk, cq, ck = _forget_consts()
    tri = np.tril(np.ones((tm, tm), np.float32))
    consts = (jnp.asarray(e_np, BF16), jnp.asarray(ec_np, F32), jnp.asarray(tri, BF16),
              jnp.asarray(pq, BF16), jnp.asarray(pk, BF16), jnp.asarray(cq, F32), jnp.asarray(ck, F32))
    inv_freq = ROPE_THETA ** (-jnp.arange(0, ROT_DIM, 2, dtype=F32) / ROT_DIM)
    invf = inv_freq.reshape(ROT_DIM // 2, 1)
    pos3 = positions.reshape(B, 1, S)
    x2d = x.reshape(B * S, D)
    row = lambda g: g.reshape(1, -1)

    for l in range(depth):
        mod = _ada_call(c, w_ada[l], b_ada[l]).reshape(B, N_MOD, D)
        wg, wu, wd = _ffn_weights(w_ff1_gate[l], w_ff1_up[l], w_ff1_down[l])
        x2d = _ffn_call(x2d, mod, row(g_pre_ff1[l]), row(g_post_ff1[l]), wg, wu, wd, row0=0, seq=S, tm=tm)

        wqkv = w_in[l][:, :QKV_COLS].astype(BF16)
        wf = jnp.pad(jnp.tile(w_in[l][:, QKV_COLS:], (1, 3)), ((0, 0), (0, LANES - 3 * N_HEADS_B))).astype(BF16)
        bfg = jnp.pad(jnp.tile(b_forget[l], 3), (0, LANES - 3 * N_HEADS_B)).reshape(1, LANES)
        qa, ka, va, qb, kb, vb, aq, ak = _proj_call(
            x2d, mod, row(g_pre_mix[l]), pos3, invf, wqkv, wf, bfg, consts, batch=B, seq=S, tm=tm)
        oa = _dil_call(qa, ka, va, batch=B, seq=S).reshape(B * S, WIDTH_A)
        ob = _fox_call(qb, kb, vb, aq, ak, batch=B, seq=S).reshape(B * S, WIDTH_B)
        wo = w_out[l].astype(BF16)
        x2d = _outproj_call(oa, ob, x2d, mod, row(g_out_a[l]), row(g_out_b[l]), wo[:WIDTH_A], wo[WIDTH_A:],
                            row(g_post_mix[l]), seq=S, tm=tm)

        wg, wu, wd = _ffn_weights(w_ff2_gate[l], w_ff2_up[l], w_ff2_down[l])
        x2d = _ffn_call(x2d, mod, row(g_pre_ff2[l]), row(g_post_ff2[l]), wg, wu, wd, row0=6, seq=S, tm=tm)
    return x2d.reshape(B, S, D)
```

```python
import functools
import math

import numpy as np
import jax
import jax.numpy as jnp
from jax import lax
from jax.experimental import pallas as pl
from jax.experimental.pallas import tpu as pltpu

F32 = jnp.float32
BF16 = jnp.bfloat16

D_MODEL = 1024
HEAD_DIM = 64
N_HEADS_A = 8
N_HEADS_B = 8
WIDTH_A = N_HEADS_A * HEAD_DIM
WIDTH_B = N_HEADS_B * HEAD_DIM
ROT_DIM = HEAD_DIM // 4
ROPE_THETA = 500000.0
D_FF = 2752
N_MOD = 9
EPS = 1e-6
ATTN_SCALE = HEAD_DIM ** -0.5
NEG = -1e30
QKV_COLS = 3 * WIDTH_A + 3 * WIDTH_B
LOG2E = math.log2(math.e)
Q_SCALE = ATTN_SCALE * LOG2E

LANES = 128
FF_CHUNK = 256
D_FF_PAD = -(-D_FF // FF_CHUNK) * FF_CHUNK
N_FF_CHUNKS = D_FF_PAD // FF_CHUNK
FFN_SUB = 512
BLK = 128
AUG = 16
VMEM_LIMIT = 56 * 1024 * 1024


def _rms(x, g):
    ms = jnp.mean(x * x, axis=-1, keepdims=True)
    return x * lax.rsqrt(ms + EPS) * g


def _split3(x):
    hi = x.astype(BF16).astype(F32)
    r1 = x - hi
    mid = r1.astype(BF16).astype(F32)
    lo = (r1 - mid).astype(BF16).astype(F32)
    return hi, mid, lo


def _const_spec(shape):
    nd = len(shape)
    return pl.BlockSpec(shape, lambda *_: (0,) * nd, pipeline_mode=pl.Buffered(1))


def _ada_kernel(c_ref, w_ref, b_ref, o_ref):
    c = c_ref[...]
    sc = (c * jax.nn.sigmoid(c)).astype(BF16)
    o_ref[...] = jnp.dot(sc, w_ref[...].astype(BF16), preferred_element_type=F32) + b_ref[...]


def _ada_call(c, w_ada, b_ada):
    B, D = c.shape
    N = w_ada.shape[1]
    tn = 1024
    return pl.pallas_call(
        _ada_kernel,
        grid=(N // tn,),
        in_specs=[pl.BlockSpec((B, D), lambda j: (0, 0)),
                  pl.BlockSpec((D, tn), lambda j: (0, j)),
                  pl.BlockSpec((1, tn), lambda j: (0, j))],
        out_specs=pl.BlockSpec((B, tn), lambda j: (0, j)),
        out_shape=jax.ShapeDtypeStruct((B, N), F32),
        compiler_params=pltpu.CompilerParams(dimension_semantics=("arbitrary",),
                                             vmem_limit_bytes=VMEM_LIMIT),
        name="adaln",
    )(c, w_ada, b_ada.reshape(1, N))


def _ffn_kernel(x_ref, mod_ref, gpre_ref, gpost_ref, wg_ref, wu_ref, wd_ref, o_ref, a_ref, *, row0):
    shift = mod_ref[0, row0:row0 + 1, :]
    gs = gpre_ref[...] * (1.0 + mod_ref[0, row0 + 1:row0 + 2, :])
    gp = gpost_ref[...] * (0.5 * mod_ref[0, row0 + 2:row0 + 3, :])
    for t in range(x_ref.shape[0] // FFN_SUB):
        rows = pl.ds(t * FFN_SUB, FFN_SUB)
        x = x_ref[rows, :]
        ms = jnp.mean(x * x, axis=-1, keepdims=True)
        hb = (x * lax.rsqrt(ms + EPS) * gs + shift).astype(BF16)
        for c in range(N_FF_CHUNKS):
            g = jnp.dot(hb, wg_ref[c], preferred_element_type=F32)
            u = jnp.dot(hb, wu_ref[c], preferred_element_type=F32)
            a_ref[t, :, c * FF_CHUNK:(c + 1) * FF_CHUNK] = (g * jax.nn.sigmoid(g) * u).astype(BF16)
        y = jnp.dot(a_ref[t], wd_ref[...], preferred_element_type=F32)
        o_ref[rows, :] = x + _rms(y, gp)


def _ffn_call(x2d, mod, g_pre, g_post, wg, wu, wd, *, row0, seq, tm):
    T, D = x2d.shape
    per_seq = seq // tm
    kern = functools.partial(_ffn_kernel, row0=row0)
    return pl.pallas_call(
        kern,
        grid=(T // tm,),
        in_specs=[pl.BlockSpec((tm, D), lambda i: (i, 0)),
                  pl.BlockSpec((1, N_MOD, D), lambda i: (i // per_seq, 0, 0)),
                  _const_spec((1, D)), _const_spec((1, D)),
                  _const_spec(wg.shape), _const_spec(wu.shape), _const_spec(wd.shape)],
        out_specs=pl.BlockSpec((tm, D), lambda i: (i, 0)),
        out_shape=jax.ShapeDtypeStruct((T, D), F32),
        scratch_shapes=[pltpu.VMEM((tm // FFN_SUB, FFN_SUB, D_FF_PAD), BF16)],
        compiler_params=pltpu.CompilerParams(dimension_semantics=("arbitrary",),
                                             vmem_limit_bytes=VMEM_LIMIT),
        name="ffn_row%d" % row0,
    )(x2d, mod, g_pre, g_post, wg, wu, wd)


def _rope_consts():
    half = ROT_DIM // 2
    e = np.zeros((3, 2 * half, 3 * LANES), np.float32)
    c = np.zeros((1, 3 * LANES), np.float32)
    for lane in range(LANES):
        d = lane % HEAD_DIM
        if d < ROT_DIM:
            e[:, d % half, lane] = 1.0
        else:
            c[0, lane] = 1.0
        if d < half:
            e[:, half + d, LANES + lane] = -1.0
        elif d < ROT_DIM:
            e[:, half + d - half, 2 * LANES + lane] = 1.0
    return e.reshape(3 * 2 * half, 3 * LANES), c


def _forget_consts():
    pq = np.zeros((LANES, LANES), np.float32)
    pk = np.zeros((LANES, LANES), np.float32)
    cq = np.zeros((1, LANES), np.float32)
    ck = np.zeros((1, LANES), np.float32)
    for h in range(N_HEADS_B):
        for g in range(3):
            pq[8 * g + h, AUG * h + g] = 1.0
            cq[0, AUG * h + 3 + g] = 1.0
            pk[8 * g + h, AUG * h + 3 + g] = -1.0
            ck[0, AUG * h + g] = 1.0
    return pq, pk, cq, ck


def _proj_kernel(x_ref, mod_ref, g_ref, pos_ref, invf_ref, wqkv_ref, wf_ref, bf_ref, e_ref, ec_ref, tri_ref,
                 pq_ref, pk_ref, cq_ref, ck_ref,
                 qa_ref, ka_ref, va_ref, qb_ref, kb_ref, vb_ref, aq_ref, ak_ref, carry_ref):
    tm = x_ref.shape[0]

    @pl.when(pl.program_id(1) == 0)
    def _():
        carry_ref[...] = jnp.zeros_like(carry_ref)

    x = x_ref[...]
    shift = mod_ref[0, 3:4, :]
    gs = g_ref[...] * (1.0 + mod_ref[0, 4:5, :])
    ms = jnp.mean(x * x, axis=-1, keepdims=True)
    hb = (x * lax.rsqrt(ms + EPS) * gs + shift).astype(BF16)

    def cols(c0, width):
        return jnp.dot(hb, wqkv_ref[:, c0:c0 + width], preferred_element_type=F32)

    def put_pairs(ref, t, fn=lambda u: u):
        for i in range(t.shape[1] // LANES):
            ref[i] = fn(t[:, i * LANES:(i + 1) * LANES]).astype(ref.dtype)


    fl = jnp.dot(hb, wf_ref[...], preferred_element_type=F32) + bf_ref[...]
    put_pairs(va_ref, cols(2 * WIDTH_A, WIDTH_A))

    posf = pos_ref[0].astype(F32)
    ang = invf_ref[...] * posf
    cs = jnp.concatenate([jnp.cos(ang), jnp.sin(ang)], axis=0)
    cs3 = jnp.concatenate([t.astype(BF16) for t in _split3(cs)], axis=0)
    tabs = lax.dot_general(cs3, e_ref[...], (((0,), (0,)), ((), ())),
                           preferred_element_type=F32) + ec_ref[...]
    t_cos = tabs[:, 0:LANES]
    t_s1 = tabs[:, LANES:2 * LANES]
    t_s2 = tabs[:, 2 * LANES:3 * LANES]
    half = ROT_DIM // 2

    def rope(t):
        return t * t_cos + pltpu.roll(t, LANES - half, 1) * t_s1 + pltpu.roll(t, half, 1) * t_s2

    lf = jnp.minimum(fl, 0.0) - jnp.log1p(jnp.exp(-jnp.abs(fl)))
    lane = lax.broadcasted_iota(jnp.int32, lf.shape, 1)
    zero = jnp.zeros_like(lf)

    def by_group(a, b, c):
        return jnp.where(lane < 8, a, jnp.where(lane < 16, b, jnp.where(lane < 24, c, zero))).astype(BF16)

    cum = jnp.dot(tri_ref[...], by_group(*_split3(lf)), preferred_element_type=F32)

    put_pairs(qa_ref, cols(0, WIDTH_A), lambda u: rope(u) * Q_SCALE)

    cum = cum + pltpu.roll(cum, LANES - 8, 1) + pltpu.roll(cum, LANES - 16, 1)
    f_tot = cum + carry_ref[...]
    carry_ref[...] = f_tot[tm - 1:tm, :]
    p1, p2, p3 = _split3(f_tot * LOG2E)
    pieces = by_group(p1, pltpu.roll(p2, 8, 1), pltpu.roll(p3, 16, 1))
    aq_ref[...] = (jnp.dot(pieces, pq_ref[...], preferred_element_type=F32) + cq_ref[...]).astype(BF16)
    ak_ref[...] = (jnp.dot(pieces, pk_ref[...], preferred_element_type=F32) + ck_ref[...]).astype(BF16)

    put_pairs(ka_ref, cols(WIDTH_A, WIDTH_A), rope)
    o = 3 * WIDTH_A
    put_pairs(qb_ref, cols(o, WIDTH_B), lambda u: u * Q_SCALE)
    put_pairs(kb_ref, cols(o + WIDTH_B, WIDTH_B))
    put_pairs(vb_ref, cols(o + 2 * WIDTH_B, WIDTH_B))


def _proj_call(x2d, mod, g_pre, pos3, invf, wqkv, wf, bfg, consts, *, batch, seq, tm):
    T, D = x2d.shape
    per_seq = seq // tm
    tok = lambda b, j: (b * per_seq + j, 0)
    npa, npb = WIDTH_A // LANES, WIDTH_B // LANES
    pairs = lambda b, j: (b, 0, j, 0)
    out_shape = ([jax.ShapeDtypeStruct((batch, npa, seq, LANES), F32)] * 3
                 + [jax.ShapeDtypeStruct((batch, npb, seq, LANES), BF16)] * 3
                 + [jax.ShapeDtypeStruct((T, LANES), BF16)] * 2)
    out_specs = ([pl.BlockSpec((None, npa, tm, LANES), pairs)] * 3
                 + [pl.BlockSpec((None, npb, tm, LANES), pairs)] * 3
                 + [pl.BlockSpec((tm, LANES), tok)] * 2)
    return pl.pallas_call(
        _proj_kernel,
        grid=(batch, per_seq),
        in_specs=[pl.BlockSpec((tm, D), tok),
                  pl.BlockSpec((1, N_MOD, D), lambda b, j: (b, 0, 0)),
                  _const_spec((1, D)),
                  pl.BlockSpec((None, 1, tm), lambda b, j: (b, 0, j)),
                  _const_spec(invf.shape), _const_spec(wqkv.shape), _const_spec(wf.shape),
                  _const_spec(bfg.shape)] + [_const_spec(t.shape) for t in consts],
        out_specs=out_specs,
        out_shape=out_shape,
        scratch_shapes=[pltpu.VMEM((1, LANES), F32)],
        compiler_params=pltpu.CompilerParams(dimension_semantics=("arbitrary", "arbitrary"),
                                             vmem_limit_bytes=VMEM_LIMIT),
        name="mixer_proj",
    )(x2d, mod, g_pre, pos3, invf, wqkv, wf, bfg, *consts)


DILATIONS = (1, 4, 16)
DIL_AHEAD = 8
PREP_ROWS = 512


def _dil_consts():
    a = np.arange(BLK)[:, None]
    j = np.arange(2 * BLK)[None, :]
    band_ok = np.where(j < BLK, j >= a, j - BLK <= a)
    causal_ok = np.arange(BLK)[None, :] <= a
    bias = lambda ok: np.where(ok, 0.0, NEG).astype(np.float32)
    return bias(band_ok), bias(causal_ok)


def _dil_kernel(q_ref, k_ref, v_ref, band_ref, causal_ref, o_ref, qp, kp, vp, f4, o_sc, l_sc):
    S = q_ref.shape[0]
    nblk = S // BLK
    first_head = lax.broadcasted_iota(jnp.int32, (BLK, LANES), 1) < HEAD_DIM
    dn = (((1,), (1,)), ((), ()))

    def put(p, a, dst, rows):
        rb = rows.astype(BF16)
        if a == 0:
            qp[p, dst, :] = rb
        elif a == 1:
            kp[p, dst, :] = rb
        else:
            fh = lax.broadcasted_iota(jnp.int32, rb.shape, 1) < HEAD_DIM
            one = jnp.ones_like(rb)
            vp[p, 0, dst, :] = jnp.where(fh, rb, one)
            vp[p, 1, dst, :] = jnp.where(fh, one, rb)

    n4, n16 = S // 4, S // 16
    for a, ref in enumerate((q_ref, k_ref, v_ref)):
        for c0 in range(0, S, PREP_ROWS):
            put(0, a, pl.ds(c0, PREP_ROWS), ref[pl.ds(c0, PREP_ROWS), :])
        for r in range(4):
            rows = ref[pl.ds(r, n4, stride=4), :]
            f4[a, pl.ds(r * n4, n4), :] = rows
            put(1, a, pl.ds(r * n4, n4), rows)
        for r4 in range(4):
            for jj in range(4):
                rows = f4[a, pl.ds(r4 * n4 + jj, n16, stride=4), :]
                put(2, a, pl.ds((4 * jj + r4) * n16, n16), rows)

    tasks = [(p, ib) for p in (2, 1, 0) for ib in range(nblk)]

    def keys_of(p, ib):
        per_cls = (S // DILATIONS[p]) // BLK
        if ib % per_cls == 0:
            return pl.ds(ib * BLK, BLK), causal_ref
        return pl.ds((ib - 1) * BLK, 2 * BLK), band_ref

    def scores(p, ib):
        q = qp[p, pl.ds(ib * BLK, BLK), :]
        zero = jnp.zeros_like(q)
        q2 = jnp.concatenate([jnp.where(first_head, q, zero), jnp.where(first_head, zero, q)], axis=0)
        keys, bias_ref = keys_of(p, ib)
        s = lax.dot_general(q2, kp[p, keys, :], dn, preferred_element_type=F32)
        bias = bias_ref[...]
        out = []
        for hd in range(2):
            sh = s[hd * BLK:(hd + 1) * BLK, :] + bias
            m = jnp.max(sh, axis=-1, keepdims=True)
            out.append((m, jnp.exp2(sh - m).astype(BF16)))
        return out

    def finish(p, ib, m0, a0, m1, a1):
        num = jnp.where(first_head, a0, a1)
        den = pltpu.roll(jnp.where(first_head, a1, a0), HEAD_DIM, 1)
        o = num / den
        lse = jnp.where(first_head, m0, m1) + jnp.log2(den)
        d = DILATIONS[p]
        if d > 1:
            cls, t0 = divmod(ib * BLK, S // d)
            dst = pl.ds(cls + t0 * d, BLK, stride=d)
            o_sc[p, dst, :] = o
            l_sc[p, dst, :] = lse
            return
        rows = pl.ds(ib * BLK, BLK)
        ls = [lse, l_sc[1, rows, :], l_sc[2, rows, :]]
        mx = jnp.maximum(jnp.maximum(ls[0], ls[1]), ls[2])
        ws = [jnp.exp2(l - mx) for l in ls]
        tot = ws[0] * o + ws[1] * o_sc[1, rows, :] + ws[2] * o_sc[2, rows, :]
        o_ref[rows, :] = tot / (ws[0] + ws[1] + ws[2])

    pending = {}
    for i in range(len(tasks) + DIL_AHEAD):
        if i < len(tasks):
            pending[i] = scores(*tasks[i])
        j = i - DIL_AHEAD
        if j >= 0:
            p, ib = tasks[j]
            (m0, e0), (m1, e1) = pending.pop(j)
            keys = keys_of(p, ib)[0]
            a0 = jnp.dot(e0, vp[p, 0, keys, :], preferred_element_type=F32)
            a1 = jnp.dot(e1, vp[p, 1, keys, :], preferred_element_type=F32)
            finish(p, ib, m0, a0, m1, a1)


def _dil_call(qa, ka, va, *, batch, seq):
    npair = WIDTH_A // LANES
    spec = pl.BlockSpec((None, None, seq, LANES), lambda b, j: (b, j, 0, 0))
    consts = [jnp.asarray(t, F32) for t in _dil_consts()]
    return pl.pallas_call(
        _dil_kernel,
        grid=(batch, npair),
        in_specs=[spec, spec, spec] + [_const_spec(t.shape) for t in consts],
        out_specs=spec,
        out_shape=jax.ShapeDtypeStruct((batch, npair, seq, LANES), F32),
        scratch_shapes=[pltpu.VMEM((3, seq, LANES), BF16), pltpu.VMEM((3, seq, LANES), BF16),
                        pltpu.VMEM((3, 2, seq, LANES), BF16), pltpu.VMEM((3, seq, LANES), F32),
                        pltpu.VMEM((3, seq, LANES), F32), pltpu.VMEM((3, seq, LANES), F32)],
        compiler_params=pltpu.CompilerParams(dimension_semantics=("arbitrary", "arbitrary"),
                                             vmem_limit_bytes=VMEM_LIMIT),
        name="dilated_attn",
    )(qa, ka, va, *consts)


FOX_T = 256
FOX_AHEAD = 2


def _fox_kernel(q_ref, k_ref, v_ref, aq_ref, ak_ref, o_ref, vaug):
    S = q_ref.shape[0]
    nq = S // FOX_T
    pair = pl.program_id(1)
    lane = lax.broadcasted_iota(jnp.int32, (FOX_T, LANES), 1)
    first_head = lane < HEAD_DIM
    row = lax.broadcasted_iota(jnp.int32, (FOX_T, FOX_T), 0)
    col = lax.broadcasted_iota(jnp.int32, (FOX_T, FOX_T), 1)
    causal = col <= row
    dn = (((1,), (1,)), ((), ()))

    fh = lax.broadcasted_iota(jnp.int32, (PREP_ROWS, LANES), 1) < HEAD_DIM
    one = jnp.ones((PREP_ROWS, LANES), BF16)
    for c0 in range(0, S, PREP_ROWS):
        v = v_ref[c0:c0 + PREP_ROWS, :]
        vaug[0, c0:c0 + PREP_ROWS, :] = jnp.where(fh, v, one)
        vaug[1, c0:c0 + PREP_ROWS, :] = jnp.where(fh, one, v)

    tasks = [(qi, hd) for qi in range(nq) for hd in range(2)]

    def scores(qi, hd):
        r0, n = qi * FOX_T, (qi + 1) * FOX_T
        q = q_ref[r0:r0 + FOX_T, :]
        aq = aq_ref[r0:r0 + FOX_T, :]
        lo = AUG * (2 * pair + hd)
        own = (lane >= lo) & (lane < lo + AUG)
        ql = jnp.concatenate(
            [jnp.where(first_head if hd == 0 else jnp.logical_not(first_head), q, jnp.zeros_like(q)),
             jnp.where(own, aq, jnp.zeros_like(aq))], axis=1)
        kk = jnp.concatenate([k_ref[0:n, :], ak_ref[0:n, :]], axis=1)
        s = lax.dot_general(ql, kk, dn, preferred_element_type=F32)
        tiles = [s[:, c:c + LANES] for c in range(0, n - FOX_T, LANES)]
        diag = jnp.where(causal, s[:, n - FOX_T:n], NEG)
        tiles += [diag[:, c:c + LANES] for c in range(0, FOX_T, LANES)]
        mx = tiles[0]
        for t in tiles[1:]:
            mx = jnp.maximum(mx, t)
        m = jnp.max(mx, axis=-1, keepdims=True)
        return jnp.concatenate([jnp.exp2(t - m).astype(BF16) for t in tiles], axis=1)

    pending = {}
    for i in range(len(tasks) + FOX_AHEAD):
        if i < len(tasks):
            pending[i] = scores(*tasks[i])
        j = i - FOX_AHEAD
        if j >= 0:
            qi, hd = tasks[j]
            n = (qi + 1) * FOX_T
            acc = jnp.dot(pending.pop(j), vaug[hd, 0:n, :], preferred_element_type=F32)
            if hd == 0:
                a0 = acc
            else:
                num = jnp.where(first_head, a0, acc)
                den = pltpu.roll(jnp.where(first_head, acc, a0), HEAD_DIM, 1)
                o_ref[qi * FOX_T:(qi + 1) * FOX_T, :] = num / den


def _fox_call(qb, kb, vb, aq, ak, *, batch, seq):
    npair = WIDTH_B // LANES
    spec = pl.BlockSpec((None, None, seq, LANES), lambda b, j: (b, j, 0, 0))
    aspec = pl.BlockSpec((None, seq, LANES), lambda b, j: (b, 0, 0))
    return pl.pallas_call(
        _fox_kernel,
        grid=(batch, npair),
        in_specs=[spec, spec, spec, aspec, aspec],
        out_specs=spec,
        out_shape=jax.ShapeDtypeStruct((batch, npair, seq, LANES), F32),
        scratch_shapes=[pltpu.VMEM((2, seq, LANES), BF16)],
        compiler_params=pltpu.CompilerParams(dimension_semantics=("arbitrary", "arbitrary"),
                                             vmem_limit_bytes=VMEM_LIMIT),
        name="forget_attn",
    )(qb, kb, vb, aq.reshape(batch, seq, LANES), ak.reshape(batch, seq, LANES))


def _outproj_kernel(oa_ref, ob_ref, x_ref, mod_ref, ga_ref, gb_ref, woa_ref, wob_ref, gpost_ref, o_ref):
    oa = jnp.concatenate([oa_ref[i] for i in range(oa_ref.shape[0])], axis=1)
    ob = jnp.concatenate([ob_ref[i] for i in range(ob_ref.shape[0])], axis=1)
    a = _rms(oa, ga_ref[...]).astype(BF16)
    b = _rms(ob, gb_ref[...]).astype(BF16)
    y = (jnp.dot(a, woa_ref[...], preferred_element_type=F32)
         + jnp.dot(b, wob_ref[...], preferred_element_type=F32))
    gate = mod_ref[0, 5:6, :]
    o_ref[...] = x_ref[...] + gate * _rms(y, gpost_ref[...])


def _outproj_call(oa, ob, x2d, mod, ga, gb, woa, wob, gpost, *, seq, tm):
    T, D = x2d.shape
    per_seq = seq // tm
    return pl.pallas_call(
        _outproj_kernel,
        grid=(T // tm,),
        in_specs=[pl.BlockSpec((None, WIDTH_A // LANES, tm, LANES), lambda i: (i // per_seq, 0, i % per_seq, 0)),
                  pl.BlockSpec((None, WIDTH_B // LANES, tm, LANES), lambda i: (i // per_seq, 0, i % per_seq, 0)),
                  pl.BlockSpec((tm, D), lambda i: (i, 0)),
                  pl.BlockSpec((1, N_MOD, D), lambda i: (i // per_seq, 0, 0)),
                  _const_spec((1, WIDTH_A)), _const_spec((1, WIDTH_B)),
                  _const_spec(woa.shape), _const_spec(wob.shape), _const_spec((1, D))],
        out_specs=pl.BlockSpec((tm, D), lambda i: (i, 0)),
        out_shape=jax.ShapeDtypeStruct((T, D), F32),
        compiler_params=pltpu.CompilerParams(dimension_semantics=("arbitrary",),
                                             vmem_limit_bytes=VMEM_LIMIT),
        name="mixer_out",
    )(oa, ob, x2d, mod, ga, gb, woa, wob, gpost)


def _ffn_weights(w_gate, w_up, w_down):
    pad = D_FF_PAD - D_FF
    cols = lambda w: jnp.pad(w, ((0, 0), (0, pad))).astype(BF16).reshape(
        D_MODEL, N_FF_CHUNKS, FF_CHUNK).transpose(1, 0, 2)
    wd = jnp.pad(w_down, ((0, pad), (0, 0))).astype(BF16)
    return cols(w_gate), cols(w_up), wd


def kernel(x, c, positions, w_ada, b_ada, g_pre_ff1, g_post_ff1, w_ff1_gate, w_ff1_up, w_ff1_down,
           g_pre_mix, g_post_mix, w_in, b_forget, g_out_a, g_out_b, w_out,
           g_pre_ff2, g_post_ff2, w_ff2_gate, w_ff2_up, w_ff2_down):
    B, S, D = x.shape
    depth = w_ada.shape[0]
    tm = 512
    e_np, ec_np = _rope_consts()
    pq, pk, cq, ck = _forget_consts()
    tri = np.tril(np.ones((tm, tm), np.float32))
    consts = (jnp.asarray(e_np, BF16), jnp.asarray(ec_np, F32), jnp.asarray(tri, BF16),
              jnp.asarray(pq, BF16), jnp.asarray(pk, BF16), jnp.asarray(cq, F32), jnp.asarray(ck, F32))
    inv_freq = ROPE_THETA ** (-jnp.arange(0, ROT_DIM, 2, dtype=F32) / ROT_DIM)
    invf = inv_freq.reshape(ROT_DIM // 2, 1)
    pos3 = positions.reshape(B, 1, S)
    x2d = x.reshape(B * S, D)
    row = lambda g: g.reshape(1, -1)

    for l in range(depth):
        mod = _ada_call(c, w_ada[l], b_ada[l]).reshape(B, N_MOD, D)
        wg, wu, wd = _ffn_weights(w_ff1_gate[l], w_ff1_up[l], w_ff1_down[l])
        x2d = _ffn_call(x2d, mod, row(g_pre_ff1[l]), row(g_post_ff1[l]), wg, wu, wd, row0=0, seq=S, tm=2 * FFN_SUB)

        wqkv = w_in[l][:, :QKV_COLS].astype(BF16)
        wf = jnp.pad(jnp.tile(w_in[l][:, QKV_COLS:], (1, 3)), ((0, 0), (0, LANES - 3 * N_HEADS_B))).astype(BF16)
        bfg = jnp.pad(jnp.tile(b_forget[l], 3), (0, LANES - 3 * N_HEADS_B)).reshape(1, LANES)
        qa, ka, va, qb, kb, vb, aq, ak = _proj_call(
            x2d, mod, row(g_pre_mix[l]), pos3, invf, wqkv, wf, bfg, consts, batch=B, seq=S, tm=tm)
        oa = _dil_call(qa, ka, va, batch=B, seq=S)
        ob = _fox_call(qb, kb, vb, aq, ak, batch=B, seq=S)
        wo = w_out[l].astype(BF16)
        x2d = _outproj_call(oa, ob, x2d, mod, row(g_out_a[l]), row(g_out_b[l]), wo[:WIDTH_A], wo[WIDTH_A:],
                            row(g_post_mix[l]), seq=S, tm=tm)

        wg, wu, wd = _ffn_weights(w_ff2_gate[l], w_ff2_up[l], w_ff2_down[l])
        x2d = _ffn_call(x2d, mod, row(g_pre_ff2[l]), row(g_post_ff2[l]), wg, wu, wd, row0=6, seq=S, tm=2 * FFN_SUB)
    return x2d.reshape(B, S, D)
```

```python
import functools
import math

import numpy as np
import jax
import jax.numpy as jnp
from jax import lax
from jax.experimental import pallas as pl
from jax.experimental.pallas import tpu as pltpu

F32 = jnp.float32
BF16 = jnp.bfloat16

D_MODEL = 1024
HEAD_DIM = 64
N_HEADS_A = 8
N_HEADS_B = 8
WIDTH_A = N_HEADS_A * HEAD_DIM
WIDTH_B = N_HEADS_B * HEAD_DIM
ROT_DIM = HEAD_DIM // 4
ROPE_THETA = 500000.0
D_FF = 2752
N_MOD = 9
EPS = 1e-6
ATTN_SCALE = HEAD_DIM ** -0.5
NEG = -1e30
QKV_COLS = 3 * WIDTH_A + 3 * WIDTH_B
LOG2E = math.log2(math.e)
Q_SCALE = ATTN_SCALE * LOG2E

LANES = 128
FF_CHUNK = 256
D_FF_PAD = -(-D_FF // FF_CHUNK) * FF_CHUNK
N_FF_CHUNKS = D_FF_PAD // FF_CHUNK
FFN_SUB = 512
BLK = 128
AUG = 16
VMEM_LIMIT = 56 * 1024 * 1024


def _rms(x, g):
    ms = jnp.mean(x * x, axis=-1, keepdims=True)
    return x * lax.rsqrt(ms + EPS) * g


def _split3(x):
    hi = x.astype(BF16).astype(F32)
    r1 = x - hi
    mid = r1.astype(BF16).astype(F32)
    lo = (r1 - mid).astype(BF16).astype(F32)
    return hi, mid, lo


def _const_spec(shape):
    nd = len(shape)
    return pl.BlockSpec(shape, lambda *_: (0,) * nd, pipeline_mode=pl.Buffered(1))


def _ada_kernel(c_ref, w_ref, b_ref, o_ref):
    c = c_ref[...]
    sc = (c * jax.nn.sigmoid(c)).astype(BF16)
    o_ref[...] = jnp.dot(sc, w_ref[...].astype(BF16), preferred_element_type=F32) + b_ref[...]


def _ada_call(c, w_ada, b_ada):
    B, D = c.shape
    N = w_ada.shape[1]
    tn = 1024
    return pl.pallas_call(
        _ada_kernel,
        grid=(N // tn,),
        in_specs=[pl.BlockSpec((B, D), lambda j: (0, 0)),
                  pl.BlockSpec((D, tn), lambda j: (0, j)),
                  pl.BlockSpec((1, tn), lambda j: (0, j))],
        out_specs=pl.BlockSpec((B, tn), lambda j: (0, j)),
        out_shape=jax.ShapeDtypeStruct((B, N), F32),
        compiler_params=pltpu.CompilerParams(dimension_semantics=("arbitrary",),
                                             vmem_limit_bytes=VMEM_LIMIT),
        name="adaln",
    )(c, w_ada, b_ada.reshape(1, N))


def _ffn_kernel(x_ref, mod_ref, gpre_ref, gpost_ref, wg_ref, wu_ref, wd_ref, o_ref, a_ref, *, row0):
    shift = mod_ref[0, row0:row0 + 1, :]
    gs = gpre_ref[...] * (1.0 + mod_ref[0, row0 + 1:row0 + 2, :])
    gp = gpost_ref[...] * (0.5 * mod_ref[0, row0 + 2:row0 + 3, :])
    for t in range(x_ref.shape[0] // FFN_SUB):
        rows = pl.ds(t * FFN_SUB, FFN_SUB)
        x = x_ref[rows, :]
        ms = jnp.mean(x * x, axis=-1, keepdims=True)
        hb = (x * lax.rsqrt(ms + EPS) * gs + shift).astype(BF16)
        for c in range(N_FF_CHUNKS):
            cs = slice(c * FF_CHUNK, (c + 1) * FF_CHUNK)
            g = jnp.dot(hb, wg_ref[:, cs], preferred_element_type=F32)
            u = jnp.dot(hb, wu_ref[:, cs], preferred_element_type=F32)
            a_ref[t, :, cs] = (g * jax.nn.sigmoid(g) * u).astype(BF16)
        y = jnp.dot(a_ref[t], wd_ref[...], preferred_element_type=F32)
        o_ref[rows, :] = x + _rms(y, gp)


def _ffn_call(x2d, mod, g_pre, g_post, wg, wu, wd, *, row0, seq, tm):
    T, D = x2d.shape
    per_seq = seq // tm
    kern = functools.partial(_ffn_kernel, row0=row0)
    return pl.pallas_call(
        kern,
        grid=(T // tm,),
        in_specs=[pl.BlockSpec((tm, D), lambda i: (i, 0)),
                  pl.BlockSpec((1, N_MOD, D), lambda i: (i // per_seq, 0, 0)),
                  _const_spec((1, D)), _const_spec((1, D)),
                  _const_spec(wg.shape), _const_spec(wu.shape), _const_spec(wd.shape)],
        out_specs=pl.BlockSpec((tm, D), lambda i: (i, 0)),
        out_shape=jax.ShapeDtypeStruct((T, D), F32),
        scratch_shapes=[pltpu.VMEM((tm // FFN_SUB, FFN_SUB, D_FF_PAD), BF16)],
        compiler_params=pltpu.CompilerParams(dimension_semantics=("arbitrary",),
                                             vmem_limit_bytes=VMEM_LIMIT),
        name="ffn_row%d" % row0,
    )(x2d, mod, g_pre, g_post, wg, wu, wd)


def _rope_consts():
    half = ROT_DIM // 2
    e = np.zeros((3, 2 * half, 3 * LANES), np.float32)
    c = np.zeros((1, 3 * LANES), np.float32)
    for lane in range(LANES):
        d = lane % HEAD_DIM
        if d < ROT_DIM:
            e[:, d % half, lane] = 1.0
        else:
            c[0, lane] = 1.0
        if d < half:
            e[:, half + d, LANES + lane] = -1.0
        elif d < ROT_DIM:
            e[:, half + d - half, 2 * LANES + lane] = 1.0
    return e.reshape(3 * 2 * half, 3 * LANES), c


def _forget_consts():
    pq = np.zeros((LANES, LANES), np.float32)
    pk = np.zeros((LANES, LANES), np.float32)
    cq = np.zeros((1, LANES), np.float32)
    ck = np.zeros((1, LANES), np.float32)
    for h in range(N_HEADS_B):
        for g in range(3):
            pq[8 * g + h, AUG * h + g] = 1.0
            cq[0, AUG * h + 3 + g] = 1.0
            pk[8 * g + h, AUG * h + 3 + g] = -1.0
            ck[0, AUG * h + g] = 1.0
    return pq, pk, cq, ck


def _proj_kernel(x_ref, mod_ref, g_ref, pos_ref, invf_ref, wqkv_ref, wf_ref, bf_ref, e_ref, ec_ref, tri_ref,
                 pq_ref, pk_ref, cq_ref, ck_ref,
                 qa_ref, ka_ref, va_ref, qb_ref, kb_ref, vb_ref, aq_ref, ak_ref, carry_ref):
    tm = x_ref.shape[0]

    @pl.when(pl.program_id(1) == 0)
    def _():
        carry_ref[...] = jnp.zeros_like(carry_ref)

    x = x_ref[...]
    shift = mod_ref[0, 3:4, :]
    gs = g_ref[...] * (1.0 + mod_ref[0, 4:5, :])
    ms = jnp.mean(x * x, axis=-1, keepdims=True)
    hb = (x * lax.rsqrt(ms + EPS) * gs + shift).astype(BF16)

    def cols(c0, width):
        return jnp.dot(hb, wqkv_ref[:, c0:c0 + width], preferred_element_type=F32)

    def put_pairs(ref, t, fn=lambda u: u):
        for i in range(t.shape[1] // LANES):
            ref[i] = fn(t[:, i * LANES:(i + 1) * LANES]).astype(ref.dtype)


    fl = jnp.dot(hb, wf_ref[...], preferred_element_type=F32) + bf_ref[...]
    put_pairs(va_ref, cols(2 * WIDTH_A, WIDTH_A))

    posf = pos_ref[0].astype(F32)
    ang = invf_ref[...] * posf
    cs = jnp.concatenate([jnp.cos(ang), jnp.sin(ang)], axis=0)
    cs3 = jnp.concatenate([t.astype(BF16) for t in _split3(cs)], axis=0)
    tabs = lax.dot_general(cs3, e_ref[...], (((0,), (0,)), ((), ())),
                           preferred_element_type=F32) + ec_ref[...]
    t_cos = tabs[:, 0:LANES]
    t_s1 = tabs[:, LANES:2 * LANES]
    t_s2 = tabs[:, 2 * LANES:3 * LANES]
    half = ROT_DIM // 2

    def rope(t):
        return t * t_cos + pltpu.roll(t, LANES - half, 1) * t_s1 + pltpu.roll(t, half, 1) * t_s2

    lf = jnp.minimum(fl, 0.0) - jnp.log1p(jnp.exp(-jnp.abs(fl)))
    lane = lax.broadcasted_iota(jnp.int32, lf.shape, 1)
    zero = jnp.zeros_like(lf)

    def by_group(a, b, c):
        return jnp.where(lane < 8, a, jnp.where(lane < 16, b, jnp.where(lane < 24, c, zero))).astype(BF16)

    cum = jnp.dot(tri_ref[...], by_group(*_split3(lf)), preferred_element_type=F32)

    put_pairs(qa_ref, cols(0, WIDTH_A), lambda u: rope(u) * Q_SCALE)

    cum = cum + pltpu.roll(cum, LANES - 8, 1) + pltpu.roll(cum, LANES - 16, 1)
    f_tot = cum + carry_ref[...]
    carry_ref[...] = f_tot[tm - 1:tm, :]
    p1, p2, p3 = _split3(f_tot * LOG2E)
    pieces = by_group(p1, pltpu.roll(p2, 8, 1), pltpu.roll(p3, 16, 1))
    aq_ref[...] = (jnp.dot(pieces, pq_ref[...], preferred_element_type=F32) + cq_ref[...]).astype(BF16)
    ak_ref[...] = (jnp.dot(pieces, pk_ref[...], preferred_element_type=F32) + ck_ref[...]).astype(BF16)

    put_pairs(ka_ref, cols(WIDTH_A, WIDTH_A), rope)
    o = 3 * WIDTH_A
    put_pairs(qb_ref, cols(o, WIDTH_B), lambda u: u * Q_SCALE)
    put_pairs(kb_ref, cols(o + WIDTH_B, WIDTH_B))
    put_pairs(vb_ref, cols(o + 2 * WIDTH_B, WIDTH_B))


def _proj_call(x2d, mod, g_pre, pos3, invf, wqkv, wf, bfg, consts, *, batch, seq, tm):
    T, D = x2d.shape
    per_seq = seq // tm
    tok = lambda b, j: (b * per_seq + j, 0)
    npa, npb = WIDTH_A // LANES, WIDTH_B // LANES
    pairs = lambda b, j: (b, 0, j, 0)
    out_shape = ([jax.ShapeDtypeStruct((batch, npa, seq, LANES), F32)] * 3
                 + [jax.ShapeDtypeStruct((batch, npb, seq, LANES), BF16)] * 3
                 + [jax.ShapeDtypeStruct((T, LANES), BF16)] * 2)
    out_specs = ([pl.BlockSpec((None, npa, tm, LANES), pairs)] * 3
                 + [pl.BlockSpec((None, npb, tm, LANES), pairs)] * 3
                 + [pl.BlockSpec((tm, LANES), tok)] * 2)
    return pl.pallas_call(
        _proj_kernel,
        grid=(batch, per_seq),
        in_specs=[pl.BlockSpec((tm, D), tok),
                  pl.BlockSpec((1, N_MOD, D), lambda b, j: (b, 0, 0)),
                  _const_spec((1, D)),
                  pl.BlockSpec((None, 1, tm), lambda b, j: (b, 0, j)),
                  _const_spec(invf.shape), _const_spec(wqkv.shape), _const_spec(wf.shape),
                  _const_spec(bfg.shape)] + [_const_spec(t.shape) for t in consts],
        out_specs=out_specs,
        out_shape=out_shape,
        scratch_shapes=[pltpu.VMEM((1, LANES), F32)],
        compiler_params=pltpu.CompilerParams(dimension_semantics=("arbitrary", "arbitrary"),
                                             vmem_limit_bytes=VMEM_LIMIT),
        name="mixer_proj",
    )(x2d, mod, g_pre, pos3, invf, wqkv, wf, bfg, *consts)


DILATIONS = (1, 4, 16)
DIL_AHEAD = 8
PREP_ROWS = 512


def _dil_consts():
    a = np.arange(BLK)[:, None]
    j = np.arange(2 * BLK)[None, :]
    band_ok = np.where(j < BLK, j >= a, j - BLK <= a)
    causal_ok = np.arange(BLK)[None, :] <= a
    bias = lambda ok: np.where(ok, 0.0, NEG).astype(np.float32)
    return bias(band_ok), bias(causal_ok)


def _dil_kernel(q_ref, k_ref, v_ref, band_ref, causal_ref, o_ref, qp, kp, vp, f4, o_sc, l_sc):
    S = q_ref.shape[0]
    nblk = S // BLK
    first_head = lax.broadcasted_iota(jnp.int32, (BLK, LANES), 1) < HEAD_DIM
    dn = (((1,), (1,)), ((), ()))

    def put(p, a, dst, rows):
        rb = rows.astype(BF16)
        if a == 0:
            qp[p, dst, :] = rb
        elif a == 1:
            kp[p, dst, :] = rb
        else:
            fh = lax.broadcasted_iota(jnp.int32, rb.shape, 1) < HEAD_DIM
            one = jnp.ones_like(rb)
            vp[p, 0, dst, :] = jnp.where(fh, rb, one)
            vp[p, 1, dst, :] = jnp.where(fh, one, rb)

    n4, n16 = S // 4, S // 16
    for a, ref in enumerate((q_ref, k_ref, v_ref)):
        for c0 in range(0, S, PREP_ROWS):
            put(0, a, pl.ds(c0, PREP_ROWS), ref[pl.ds(c0, PREP_ROWS), :])
        for r in range(4):
            rows = ref[pl.ds(r, n4, stride=4), :]
            f4[a, pl.ds(r * n4, n4), :] = rows
            put(1, a, pl.ds(r * n4, n4), rows)
        for r4 in range(4):
            for jj in range(4):
                rows = f4[a, pl.ds(r4 * n4 + jj, n16, stride=4), :]
                put(2, a, pl.ds((4 * jj + r4) * n16, n16), rows)

    tasks = [(p, ib) for p in (2, 1, 0) for ib in range(nblk)]

    def keys_of(p, ib):
        per_cls = (S // DILATIONS[p]) // BLK
        if ib % per_cls == 0:
            return pl.ds(ib * BLK, BLK), causal_ref
        return pl.ds((ib - 1) * BLK, 2 * BLK), band_ref

    def scores(p, ib):
        q = qp[p, pl.ds(ib * BLK, BLK), :]
        zero = jnp.zeros_like(q)
        q2 = jnp.concatenate([jnp.where(first_head, q, zero), jnp.where(first_head, zero, q)], axis=0)
        keys, bias_ref = keys_of(p, ib)
        s = lax.dot_general(q2, kp[p, keys, :], dn, preferred_element_type=F32)
        bias = bias_ref[...]
        out = []
        for hd in range(2):
            sh = s[hd * BLK:(hd + 1) * BLK, :] + bias
            m = jnp.max(sh, axis=-1, keepdims=True)
            out.append((m, jnp.exp2(sh - m).astype(BF16)))
        return out

    def merge(o, lse, o2, lse2):
        mx = jnp.maximum(lse, lse2)
        w, w2 = jnp.exp2(lse - mx), jnp.exp2(lse2 - mx)
        tot = w + w2
        return (w * o + w2 * o2) / tot, mx + jnp.log2(tot)

    def finish(p, ib, m0, a0, m1, a1):
        num = jnp.where(first_head, a0, a1)
        den = pltpu.roll(jnp.where(first_head, a1, a0), HEAD_DIM, 1)
        o = num / den
        lse = jnp.where(first_head, m0, m1) + jnp.log2(den)
        rows = pl.ds(ib * BLK, BLK)
        n4 = S // 4
        if p == 2:
            jj, r4 = divmod(ib, 4)
            dst = pl.ds(r4 * n4 + jj, BLK, stride=4)
            o_sc[0, dst, :] = o
            l_sc[0, dst, :] = lse
        elif p == 1:
            o, lse = merge(o, lse, o_sc[0, rows, :], l_sc[0, rows, :])
            cls, t0 = divmod(ib * BLK, n4)
            dst = pl.ds(cls + t0 * 4, BLK, stride=4)
            o_sc[1, dst, :] = o
            l_sc[1, dst, :] = lse
        else:
            o_ref[rows, :] = merge(o, lse, o_sc[1, rows, :], l_sc[1, rows, :])[0].astype(o_ref.dtype)

    pending = {}
    for i in range(len(tasks) + DIL_AHEAD):
        if i < len(tasks):
            pending[i] = scores(*tasks[i])
        j = i - DIL_AHEAD
        if j >= 0:
            p, ib = tasks[j]
            (m0, e0), (m1, e1) = pending.pop(j)
            keys = keys_of(p, ib)[0]
            a0 = jnp.dot(e0, vp[p, 0, keys, :], preferred_element_type=F32)
            a1 = jnp.dot(e1, vp[p, 1, keys, :], preferred_element_type=F32)
            finish(p, ib, m0, a0, m1, a1)


def _dil_call(qa, ka, va, *, batch, seq):
    npair = WIDTH_A // LANES
    spec = pl.BlockSpec((None, None, seq, LANES), lambda b, j: (b, j, 0, 0))
    consts = [jnp.asarray(t, F32) for t in _dil_consts()]
    return pl.pallas_call(
        _dil_kernel,
        grid=(batch, npair),
        in_specs=[spec, spec, spec] + [_const_spec(t.shape) for t in consts],
        out_specs=spec,
        out_shape=jax.ShapeDtypeStruct((batch, npair, seq, LANES), BF16),
        scratch_shapes=[pltpu.VMEM((3, seq, LANES), BF16), pltpu.VMEM((3, seq, LANES), BF16),
                        pltpu.VMEM((3, 2, seq, LANES), BF16), pltpu.VMEM((3, seq, LANES), F32),
                        pltpu.VMEM((2, seq, LANES), F32), pltpu.VMEM((2, seq, LANES), F32)],
        compiler_params=pltpu.CompilerParams(dimension_semantics=("arbitrary", "arbitrary"),
                                             vmem_limit_bytes=VMEM_LIMIT),
        name="dilated_attn",
    )(qa, ka, va, *consts)


FOX_T = 256
FOX_AHEAD = 3


def _fox_kernel(q_ref, k_ref, v_ref, aq_ref, ak_ref, o_ref, vaug):
    S = q_ref.shape[0]
    nq = S // FOX_T
    pair = pl.program_id(1)
    lane = lax.broadcasted_iota(jnp.int32, (FOX_T, LANES), 1)
    first_head = lane < HEAD_DIM
    row = lax.broadcasted_iota(jnp.int32, (FOX_T, FOX_T), 0)
    col = lax.broadcasted_iota(jnp.int32, (FOX_T, FOX_T), 1)
    causal = col <= row
    dn = (((1,), (1,)), ((), ()))

    fh = lax.broadcasted_iota(jnp.int32, (PREP_ROWS, LANES), 1) < HEAD_DIM
    one = jnp.ones((PREP_ROWS, LANES), BF16)
    for c0 in range(0, S, PREP_ROWS):
        v = v_ref[c0:c0 + PREP_ROWS, :]
        vaug[0, c0:c0 + PREP_ROWS, :] = jnp.where(fh, v, one)
        vaug[1, c0:c0 + PREP_ROWS, :] = jnp.where(fh, one, v)

    tasks = [(qi, hd) for qi in range(nq) for hd in range(2)]

    def scores(qi, hd):
        r0, n = qi * FOX_T, (qi + 1) * FOX_T
        q = q_ref[r0:r0 + FOX_T, :]
        aq = aq_ref[r0:r0 + FOX_T, :]
        lo = AUG * (2 * pair + hd)
        own = (lane >= lo) & (lane < lo + AUG)
        ql = jnp.concatenate(
            [jnp.where(first_head if hd == 0 else jnp.logical_not(first_head), q, jnp.zeros_like(q)),
             jnp.where(own, aq, jnp.zeros_like(aq))], axis=1)
        kk = jnp.concatenate([k_ref[0:n, :], ak_ref[0:n, :]], axis=1)
        s = lax.dot_general(ql, kk, dn, preferred_element_type=F32)
        tiles = [s[:, c:c + LANES] for c in range(0, n - FOX_T, LANES)]
        diag = jnp.where(causal, s[:, n - FOX_T:n], NEG)
        tiles += [diag[:, c:c + LANES] for c in range(0, FOX_T, LANES)]
        mx = tiles[0]
        for t in tiles[1:]:
            mx = jnp.maximum(mx, t)
        m = jnp.max(mx, axis=-1, keepdims=True)
        return jnp.concatenate([jnp.exp2(t - m).astype(BF16) for t in tiles], axis=1)

    pending = {}
    for i in range(len(tasks) + FOX_AHEAD):
        if i < len(tasks):
            pending[i] = scores(*tasks[i])
        j = i - FOX_AHEAD
        if j >= 0:
            qi, hd = tasks[j]
            n = (qi + 1) * FOX_T
            acc = jnp.dot(pending.pop(j), vaug[hd, 0:n, :], preferred_element_type=F32)
            if hd == 0:
                a0 = acc
            else:
                num = jnp.where(first_head, a0, acc)
                den = pltpu.roll(jnp.where(first_head, acc, a0), HEAD_DIM, 1)
                o_ref[qi * FOX_T:(qi + 1) * FOX_T, :] = (num / den).astype(o_ref.dtype)


def _fox_call(qb, kb, vb, aq, ak, *, batch, seq):
    npair = WIDTH_B // LANES
    spec = pl.BlockSpec((None, None, seq, LANES), lambda b, j: (b, j, 0, 0))
    aspec = pl.BlockSpec((None, seq, LANES), lambda b, j: (b, 0, 0))
    return pl.pallas_call(
        _fox_kernel,
        grid=(batch, npair),
        in_specs=[spec, spec, spec, aspec, aspec],
        out_specs=spec,
        out_shape=jax.ShapeDtypeStruct((batch, npair, seq, LANES), BF16),
        scratch_shapes=[pltpu.VMEM((2, seq, LANES), BF16)],
        compiler_params=pltpu.CompilerParams(dimension_semantics=("arbitrary", "arbitrary"),
                                             vmem_limit_bytes=VMEM_LIMIT),
        name="forget_attn",
    )(qb, kb, vb, aq.reshape(batch, seq, LANES), ak.reshape(batch, seq, LANES))


def _outproj_kernel(oa_ref, ob_ref, x_ref, mod_ref, ga_ref, gb_ref, wo_ref, gpost_ref, o_ref):
    oa = jnp.concatenate([oa_ref[i] for i in range(oa_ref.shape[0])], axis=1).astype(F32)
    ob = jnp.concatenate([ob_ref[i] for i in range(ob_ref.shape[0])], axis=1).astype(F32)
    a = _rms(oa, ga_ref[...]).astype(BF16)
    b = _rms(ob, gb_ref[...]).astype(BF16)
    y = (jnp.dot(a, wo_ref[0:WIDTH_A, :], preferred_element_type=F32)
         + jnp.dot(b, wo_ref[WIDTH_A:WIDTH_A + WIDTH_B, :], preferred_element_type=F32))
    o_ref[...] = x_ref[...] + _rms(y, gpost_ref[...] * mod_ref[0, 5:6, :])


def _outproj_call(oa, ob, x2d, mod, ga, gb, wo, gpost, *, seq, tm):
    T, D = x2d.shape
    per_seq = seq // tm
    return pl.pallas_call(
        _outproj_kernel,
        grid=(T // tm,),
        in_specs=[pl.BlockSpec((None, WIDTH_A // LANES, tm, LANES), lambda i: (i // per_seq, 0, i % per_seq, 0)),
                  pl.BlockSpec((None, WIDTH_B // LANES, tm, LANES), lambda i: (i // per_seq, 0, i % per_seq, 0)),
                  pl.BlockSpec((tm, D), lambda i: (i, 0)),
                  pl.BlockSpec((1, N_MOD, D), lambda i: (i // per_seq, 0, 0)),
                  _const_spec((1, WIDTH_A)), _const_spec((1, WIDTH_B)),
                  _const_spec(wo.shape), _const_spec((1, D))],
        out_specs=pl.BlockSpec((tm, D), lambda i: (i, 0)),
        out_shape=jax.ShapeDtypeStruct((T, D), F32),
        compiler_params=pltpu.CompilerParams(dimension_semantics=("arbitrary",),
                                             vmem_limit_bytes=VMEM_LIMIT),
        name="mixer_out",
    )(oa, ob, x2d, mod, ga, gb, wo, gpost)


def _ffn_weights(w_gate, w_up, w_down):
    pad = D_FF_PAD - D_FF
    cols = lambda w: jnp.pad(w, ((0, 0), (0, pad))).astype(BF16)
    wd = jnp.pad(w_down, ((0, pad), (0, 0))).astype(BF16)
    return cols(w_gate), cols(w_up), wd


def kernel(x, c, positions, w_ada, b_ada, g_pre_ff1, g_post_ff1, w_ff1_gate, w_ff1_up, w_ff1_down,
           g_pre_mix, g_post_mix, w_in, b_forget, g_out_a, g_out_b, w_out,
           g_pre_ff2, g_post_ff2, w_ff2_gate, w_ff2_up, w_ff2_down):
    B, S, D = x.shape
    depth = w_ada.shape[0]
    tm = 512
    e_np, ec_np = _rope_consts()
    pq, pk, cq, ck = _forget_consts()
    tri = np.tril(np.ones((tm, tm), np.float32))
    consts = (jnp.asarray(e_np, BF16), jnp.asarray(ec_np, F32), jnp.asarray(tri, BF16),
              jnp.asarray(pq, BF16), jnp.asarray(pk, BF16), jnp.asarray(cq, F32), jnp.asarray(ck, F32))
    inv_freq = ROPE_THETA ** (-jnp.arange(0, ROT_DIM, 2, dtype=F32) / ROT_DIM)
    invf = inv_freq.reshape(ROT_DIM // 2, 1)
    pos3 = positions.reshape(B, 1, S)
    x2d = x.reshape(B * S, D)
    row = lambda g: g.reshape(1, -1)

    for l in range(depth):
        mod = _ada_call(c, w_ada[l], b_ada[l]).reshape(B, N_MOD, D)
        wg, wu, wd = _ffn_weights(w_ff1_gate[l], w_ff1_up[l], w_ff1_down[l])
        x2d = _ffn_call(x2d, mod, row(g_pre_ff1[l]), row(g_post_ff1[l]), wg, wu, wd, row0=0, seq=S, tm=2 * FFN_SUB)

        wqkv = w_in[l].astype(BF16)
        wf = jnp.pad(jnp.tile(w_in[l][:, QKV_COLS:], (1, 3)), ((0, 0), (0, LANES - 3 * N_HEADS_B))).astype(BF16)
        bfg = jnp.pad(jnp.tile(b_forget[l], 3), (0, LANES - 3 * N_HEADS_B)).reshape(1, LANES)
        qa, ka, va, qb, kb, vb, aq, ak = _proj_call(
            x2d, mod, row(g_pre_mix[l]), pos3, invf, wqkv, wf, bfg, consts, batch=B, seq=S, tm=tm)
        oa = _dil_call(qa, ka, va, batch=B, seq=S)
        ob = _fox_call(qb, kb, vb, aq, ak, batch=B, seq=S)
        wo = w_out[l].astype(BF16)
        x2d = _outproj_call(oa, ob, x2d, mod, row(g_out_a[l]), row(g_out_b[l]), wo,
                            row(g_post_mix[l]), seq=S, tm=tm)

        wg, wu, wd = _ffn_weights(w_ff2_gate[l], w_ff2_up[l], w_ff2_down[l])
        x2d = _ffn_call(x2d, mod, row(g_pre_ff2[l]), row(g_post_ff2[l]), wg, wu, wd, row0=6, seq=S, tm=2 * FFN_SUB)
    return x2d.reshape(B, S, D)
```

```python
import functools
import math

import numpy as np
import jax
import jax.numpy as jnp
from jax import lax
from jax.experimental import pallas as pl
from jax.experimental.pallas import tpu as pltpu

F32 = jnp.float32
BF16 = jnp.bfloat16

D_MODEL = 1024
HEAD_DIM = 64
N_HEADS_A = 8
N_HEADS_B = 8
WIDTH_A = N_HEADS_A * HEAD_DIM
WIDTH_B = N_HEADS_B * HEAD_DIM
ROT_DIM = HEAD_DIM // 4
ROPE_THETA = 500000.0
D_FF = 2752
N_MOD = 9
EPS = 1e-6
ATTN_SCALE = HEAD_DIM ** -0.5
NEG = -1e30
QKV_COLS = 3 * WIDTH_A + 3 * WIDTH_B
LOG2E = math.log2(math.e)
Q_SCALE = ATTN_SCALE * LOG2E

LANES = 128
FF_CHUNK = 256
D_FF_PAD = -(-D_FF // FF_CHUNK) * FF_CHUNK
N_FF_CHUNKS = D_FF_PAD // FF_CHUNK
FFN_SUB = 512
BLK = 128
AUG = 16
VMEM_LIMIT = 56 * 1024 * 1024


def _rms(x, g):
    ms = jnp.mean(x * x, axis=-1, keepdims=True)
    return x * lax.rsqrt(ms + EPS) * g


def _split3(x):
    hi = x.astype(BF16).astype(F32)
    r1 = x - hi
    mid = r1.astype(BF16).astype(F32)
    lo = (r1 - mid).astype(BF16).astype(F32)
    return hi, mid, lo


def _const_spec(shape):
    nd = len(shape)
    return pl.BlockSpec(shape, lambda *_: (0,) * nd, pipeline_mode=pl.Buffered(1))


def _ada_kernel(c_ref, w_ref, b_ref, o_ref):
    c = c_ref[...]
    sc = (c * jax.nn.sigmoid(c)).astype(BF16)
    o_ref[...] = jnp.dot(sc, w_ref[...].astype(BF16), preferred_element_type=F32) + b_ref[...]


def _ada_call(c, w_ada, b_ada):
    B, D = c.shape
    N = w_ada.shape[1]
    tn = 1024
    return pl.pallas_call(
        _ada_kernel,
        grid=(N // tn,),
        in_specs=[pl.BlockSpec((B, D), lambda j: (0, 0)),
                  pl.BlockSpec((D, tn), lambda j: (0, j)),
                  pl.BlockSpec((1, tn), lambda j: (0, j))],
        out_specs=pl.BlockSpec((B, tn), lambda j: (0, j)),
        out_shape=jax.ShapeDtypeStruct((B, N), F32),
        compiler_params=pltpu.CompilerParams(dimension_semantics=("arbitrary",),
                                             vmem_limit_bytes=VMEM_LIMIT),
        name="adaln",
    )(c, w_ada, b_ada.reshape(1, N))


def _ffn_kernel(*refs, row0, with_mix):
    if with_mix:
        oa_ref, ob_ref, ga_ref, gb_ref, wo_ref, gmix_ref = refs[:6]
        refs = refs[6:]
    x_ref, mod_ref, gpre_ref, gpost_ref, wg_ref, wu_ref, wd_ref, o_ref, a_ref = refs
    shift = mod_ref[0, row0:row0 + 1, :]
    gs = gpre_ref[...] * (1.0 + mod_ref[0, row0 + 1:row0 + 2, :])
    gp = gpost_ref[...] * (0.5 * mod_ref[0, row0 + 2:row0 + 3, :])
    subs = [pl.ds(t * FFN_SUB, FFN_SUB) for t in range(x_ref.shape[0] // FFN_SUB)]
    xs = [x_ref[rows, :] for rows in subs]

    if with_mix:
        gm = gmix_ref[...] * mod_ref[0, 5:6, :]
        for t, rows in enumerate(subs):
            oa = jnp.concatenate([oa_ref[i, rows, :] for i in range(oa_ref.shape[0])], axis=1).astype(F32)
            ob = jnp.concatenate([ob_ref[i, rows, :] for i in range(ob_ref.shape[0])], axis=1).astype(F32)
            a = _rms(oa, ga_ref[...]).astype(BF16)
            b = _rms(ob, gb_ref[...]).astype(BF16)
            y = (jnp.dot(a, wo_ref[0:WIDTH_A, :], preferred_element_type=F32)
                 + jnp.dot(b, wo_ref[WIDTH_A:WIDTH_A + WIDTH_B, :], preferred_element_type=F32))
            xs[t] = xs[t] + _rms(y, gm)

    for t, rows in enumerate(subs):
        x = xs[t]
        ms = jnp.mean(x * x, axis=-1, keepdims=True)
        hb = (x * lax.rsqrt(ms + EPS) * gs + shift).astype(BF16)
        for c in range(N_FF_CHUNKS):
            cs = slice(c * FF_CHUNK, (c + 1) * FF_CHUNK)
            g = jnp.dot(hb, wg_ref[:, cs], preferred_element_type=F32)
            u = jnp.dot(hb, wu_ref[:, cs], preferred_element_type=F32)
            a_ref[t, :, cs] = (g * jax.nn.sigmoid(g) * u).astype(BF16)
        y = jnp.dot(a_ref[t], wd_ref[...], preferred_element_type=F32)
        o_ref[rows, :] = x + _rms(y, gp)


def _ffn_call(x2d, mod, g_pre, g_post, wg, wu, wd, *, row0, seq, tm, mix=None):
    T, D = x2d.shape
    per_seq = seq // tm
    kern = functools.partial(_ffn_kernel, row0=row0, with_mix=mix is not None)
    args, specs = [], []
    if mix is not None:
        oa, ob, ga, gb, wo, gmix = mix
        pair_spec = lambda t: pl.BlockSpec((None, t.shape[1], tm, LANES),
                                           lambda i: (i // per_seq, 0, i % per_seq, 0))
        args += [oa, ob, ga, gb, wo, gmix]
        specs += [pair_spec(oa), pair_spec(ob), _const_spec(ga.shape), _const_spec(gb.shape),
                  _const_spec(wo.shape), _const_spec(gmix.shape)]
    args += [x2d, mod, g_pre, g_post, wg, wu, wd]
    specs += [pl.BlockSpec((tm, D), lambda i: (i, 0)),
              pl.BlockSpec((1, N_MOD, D), lambda i: (i // per_seq, 0, 0)),
              _const_spec((1, D)), _const_spec((1, D)),
              _const_spec(wg.shape), _const_spec(wu.shape), _const_spec(wd.shape)]
    return pl.pallas_call(
        kern,
        grid=(T // tm,),
        in_specs=specs,
        out_specs=pl.BlockSpec((tm, D), lambda i: (i, 0)),
        out_shape=jax.ShapeDtypeStruct((T, D), F32),
        scratch_shapes=[pltpu.VMEM((tm // FFN_SUB, FFN_SUB, D_FF_PAD), BF16)],
        compiler_params=pltpu.CompilerParams(dimension_semantics=("arbitrary",),
                                             vmem_limit_bytes=VMEM_LIMIT),
        name="ffn_row%d" % row0,
    )(*args)


def _rope_consts():
    half = ROT_DIM // 2
    e = np.zeros((3, 2 * half, 3 * LANES), np.float32)
    c = np.zeros((1, 3 * LANES), np.float32)
    for lane in range(LANES):
        d = lane % HEAD_DIM
        if d < ROT_DIM:
            e[:, d % half, lane] = 1.0
        else:
            c[0, lane] = 1.0
        if d < half:
            e[:, half + d, LANES + lane] = -1.0
        elif d < ROT_DIM:
            e[:, half + d - half, 2 * LANES + lane] = 1.0
    return e.reshape(3 * 2 * half, 3 * LANES), c


def _forget_consts():
    pq = np.zeros((LANES, LANES), np.float32)
    pk = np.zeros((LANES, LANES), np.float32)
    cq = np.zeros((1, LANES), np.float32)
    ck = np.zeros((1, LANES), np.float32)
    for h in range(N_HEADS_B):
        for g in range(3):
            pq[8 * g + h, AUG * h + g] = 1.0
            cq[0, AUG * h + 3 + g] = 1.0
            pk[8 * g + h, AUG * h + 3 + g] = -1.0
            ck[0, AUG * h + g] = 1.0
    return pq, pk, cq, ck


def _proj_kernel(x_ref, mod_ref, g_ref, pos_ref, invf_ref, wqkv_ref, wf_ref, bf_ref, e_ref, ec_ref, tri_ref,
                 pq_ref, pk_ref, cq_ref, ck_ref,
                 qa_ref, ka_ref, va_ref, qb_ref, kb_ref, vb_ref, aq_ref, ak_ref, carry_ref):
    tm = x_ref.shape[0]

    @pl.when(pl.program_id(1) == 0)
    def _():
        carry_ref[...] = jnp.zeros_like(carry_ref)

    x = x_ref[...]
    shift = mod_ref[0, 3:4, :]
    gs = g_ref[...] * (1.0 + mod_ref[0, 4:5, :])
    ms = jnp.mean(x * x, axis=-1, keepdims=True)
    hb = (x * lax.rsqrt(ms + EPS) * gs + shift).astype(BF16)

    def cols(c0, width):
        return jnp.dot(hb, wqkv_ref[:, c0:c0 + width], preferred_element_type=F32)

    def put_pairs(ref, t, fn=lambda u: u):
        for i in range(t.shape[1] // LANES):
            ref[i] = fn(t[:, i * LANES:(i + 1) * LANES]).astype(ref.dtype)


    fl = jnp.dot(hb, wf_ref[...], preferred_element_type=F32) + bf_ref[...]
    put_pairs(va_ref, cols(2 * WIDTH_A, WIDTH_A))

    posf = pos_ref[0].astype(F32)
    ang = invf_ref[...] * posf
    cs = jnp.concatenate([jnp.cos(ang), jnp.sin(ang)], axis=0)
    cs3 = jnp.concatenate([t.astype(BF16) for t in _split3(cs)], axis=0)
    tabs = lax.dot_general(cs3, e_ref[...], (((0,), (0,)), ((), ())),
                           preferred_element_type=F32) + ec_ref[...]
    t_cos = tabs[:, 0:LANES]
    t_s1 = tabs[:, LANES:2 * LANES]
    t_s2 = tabs[:, 2 * LANES:3 * LANES]
    half = ROT_DIM // 2

    def rope(t):
        return t * t_cos + pltpu.roll(t, LANES - half, 1) * t_s1 + pltpu.roll(t, half, 1) * t_s2

    lf = jnp.minimum(fl, 0.0) - jnp.log1p(jnp.exp(-jnp.abs(fl)))
    lane = lax.broadcasted_iota(jnp.int32, lf.shape, 1)
    zero = jnp.zeros_like(lf)

    def by_group(a, b, c):
        return jnp.where(lane < 8, a, jnp.where(lane < 16, b, jnp.where(lane < 24, c, zero))).astype(BF16)

    cum = jnp.dot(tri_ref[...], by_group(*_split3(lf)), preferred_element_type=F32)

    put_pairs(qa_ref, cols(0, WIDTH_A), lambda u: rope(u) * Q_SCALE)

    cum = cum + pltpu.roll(cum, LANES - 8, 1) + pltpu.roll(cum, LANES - 16, 1)
    f_tot = cum + carry_ref[...]
    carry_ref[...] = f_tot[tm - 1:tm, :]
    p1, p2, p3 = _split3(f_tot * LOG2E)
    pieces = by_group(p1, pltpu.roll(p2, 8, 1), pltpu.roll(p3, 16, 1))
    aq_ref[...] = (jnp.dot(pieces, pq_ref[...], preferred_element_type=F32) + cq_ref[...]).astype(BF16)
    ak_ref[...] = (jnp.dot(pieces, pk_ref[...], preferred_element_type=F32) + ck_ref[...]).astype(BF16)

    put_pairs(ka_ref, cols(WIDTH_A, WIDTH_A), rope)
    o = 3 * WIDTH_A
    put_pairs(qb_ref, cols(o, WIDTH_B), lambda u: u * Q_SCALE)
    put_pairs(kb_ref, cols(o + WIDTH_B, WIDTH_B))
    put_pairs(vb_ref, cols(o + 2 * WIDTH_B, WIDTH_B))


def _proj_call(x2d, mod, g_pre, pos3, invf, wqkv, wf, bfg, consts, *, batch, seq, tm):
    T, D = x2d.shape
    per_seq = seq // tm
    tok = lambda b, j: (b * per_seq + j, 0)
    npa, npb = WIDTH_A // LANES, WIDTH_B // LANES
    pairs = lambda b, j: (b, 0, j, 0)
    out_shape = ([jax.ShapeDtypeStruct((batch, npa, seq, LANES), F32)] * 3
                 + [jax.ShapeDtypeStruct((batch, npb, seq, LANES), BF16)] * 3
                 + [jax.ShapeDtypeStruct((T, LANES), BF16)] * 2)
    out_specs = ([pl.BlockSpec((None, npa, tm, LANES), pairs)] * 3
                 + [pl.BlockSpec((None, npb, tm, LANES), pairs)] * 3
                 + [pl.BlockSpec((tm, LANES), tok)] * 2)
    return pl.pallas_call(
        _proj_kernel,
        grid=(batch, per_seq),
        in_specs=[pl.BlockSpec((tm, D), tok),
                  pl.BlockSpec((1, N_MOD, D), lambda b, j: (b, 0, 0)),
                  _const_spec((1, D)),
                  pl.BlockSpec((None, 1, tm), lambda b, j: (b, 0, j)),
                  _const_spec(invf.shape), _const_spec(wqkv.shape), _const_spec(wf.shape),
                  _const_spec(bfg.shape)] + [_const_spec(t.shape) for t in consts],
        out_specs=out_specs,
        out_shape=out_shape,
        scratch_shapes=[pltpu.VMEM((1, LANES), F32)],
        compiler_params=pltpu.CompilerParams(dimension_semantics=("arbitrary", "arbitrary"),
                                             vmem_limit_bytes=VMEM_LIMIT),
        name="mixer_proj",
    )(x2d, mod, g_pre, pos3, invf, wqkv, wf, bfg, *consts)


DILATIONS = (1, 4, 16)
DIL_AHEAD = 8
PREP_ROWS = 512


def _dil_consts():
    a = np.arange(BLK)[:, None]
    j = np.arange(2 * BLK)[None, :]
    band_ok = np.where(j < BLK, j >= a, j - BLK <= a)
    causal_ok = np.arange(BLK)[None, :] <= a
    bias = lambda ok: np.where(ok, 0.0, NEG).astype(np.float32)
    return bias(band_ok), bias(causal_ok)


def _dil_kernel(q_ref, k_ref, v_ref, band_ref, causal_ref, o_ref, qp, kp, vp, f4, o_sc, l_sc):
    S = q_ref.shape[0]
    nblk = S // BLK
    first_head = lax.broadcasted_iota(jnp.int32, (BLK, LANES), 1) < HEAD_DIM
    dn = (((1,), (1,)), ((), ()))

    def put(p, a, dst, rows):
        rb = rows.astype(BF16)
        if a == 0:
            qp[p, dst, :] = rb
        elif a == 1:
            kp[p, dst, :] = rb
        else:
            fh = lax.broadcasted_iota(jnp.int32, rb.shape, 1) < HEAD_DIM
            one = jnp.ones_like(rb)
            vp[p, 0, dst, :] = jnp.where(fh, rb, one)
            vp[p, 1, dst, :] = jnp.where(fh, one, rb)

    n4, n16 = S // 4, S // 16
    for a, ref in enumerate((q_ref, k_ref, v_ref)):
        for c0 in range(0, S, PREP_ROWS):
            put(0, a, pl.ds(c0, PREP_ROWS), ref[pl.ds(c0, PREP_ROWS), :])
        for r in range(4):
            rows = ref[pl.ds(r, n4, stride=4), :]
            f4[a, pl.ds(r * n4, n4), :] = rows
            put(1, a, pl.ds(r * n4, n4), rows)
        for r4 in range(4):
            for jj in range(4):
                rows = f4[a, pl.ds(r4 * n4 + jj, n16, stride=4), :]
                put(2, a, pl.ds((4 * jj + r4) * n16, n16), rows)

    tasks = [(p, ib) for p in (2, 1, 0) for ib in range(nblk)]

    def keys_of(p, ib):
        per_cls = (S // DILATIONS[p]) // BLK
        if ib % per_cls == 0:
            return pl.ds(ib * BLK, BLK), causal_ref
        return pl.ds((ib - 1) * BLK, 2 * BLK), band_ref

    def scores(p, ib):
        q = qp[p, pl.ds(ib * BLK, BLK), :]
        zero = jnp.zeros_like(q)
        q2 = jnp.concatenate([jnp.where(first_head, q, zero), jnp.where(first_head, zero, q)], axis=0)
        keys, bias_ref = keys_of(p, ib)
        s = lax.dot_general(q2, kp[p, keys, :], dn, preferred_element_type=F32)
        bias = bias_ref[...]
        out = []
        for hd in range(2):
            sh = s[hd * BLK:(hd + 1) * BLK, :] + bias
            m = jnp.max(sh, axis=-1, keepdims=True)
            out.append((m, jnp.exp2(sh - m).astype(BF16)))
        return out

    def merge(o, lse, o2, lse2):
        mx = jnp.maximum(lse, lse2)
        w, w2 = jnp.exp2(lse - mx), jnp.exp2(lse2 - mx)
        tot = w + w2
        return (w * o + w2 * o2) / tot, mx + jnp.log2(tot)

    def finish(p, ib, m0, a0, m1, a1):
        num = jnp.where(first_head, a0, a1)
        den = pltpu.roll(jnp.where(first_head, a1, a0), HEAD_DIM, 1)
        o = num / den
        lse = jnp.where(first_head, m0, m1) + jnp.log2(den)
        rows = pl.ds(ib * BLK, BLK)
        n4 = S // 4
        if p == 2:
            jj, r4 = divmod(ib, 4)
            dst = pl.ds(r4 * n4 + jj, BLK, stride=4)
            o_sc[0, dst, :] = o
            l_sc[0, dst, :] = lse
        elif p == 1:
            o, lse = merge(o, lse, o_sc[0, rows, :], l_sc[0, rows, :])
            cls, t0 = divmod(ib * BLK, n4)
            dst = pl.ds(cls + t0 * 4, BLK, stride=4)
            o_sc[1, dst, :] = o
            l_sc[1, dst, :] = lse
        else:
            o_ref[rows, :] = merge(o, lse, o_sc[1, rows, :], l_sc[1, rows, :])[0].astype(o_ref.dtype)

    pending = {}
    for i in range(len(tasks) + DIL_AHEAD):
        if i < len(tasks):
            pending[i] = scores(*tasks[i])
        j = i - DIL_AHEAD
        if j >= 0:
            p, ib = tasks[j]
            (m0, e0), (m1, e1) = pending.pop(j)
            keys = keys_of(p, ib)[0]
            a0 = jnp.dot(e0, vp[p, 0, keys, :], preferred_element_type=F32)
            a1 = jnp.dot(e1, vp[p, 1, keys, :], preferred_element_type=F32)
            finish(p, ib, m0, a0, m1, a1)


def _dil_call(qa, ka, va, *, batch, seq):
    npair = WIDTH_A // LANES
    spec = pl.BlockSpec((None, None, seq, LANES), lambda b, j: (b, j, 0, 0))
    consts = [jnp.asarray(t, F32) for t in _dil_consts()]
    return pl.pallas_call(
        _dil_kernel,
        grid=(batch, npair),
        in_specs=[spec, spec, spec] + [_const_spec(t.shape) for t in consts],
        out_specs=spec,
        out_shape=jax.ShapeDtypeStruct((batch, npair, seq, LANES), BF16),
        scratch_shapes=[pltpu.VMEM((3, seq, LANES), BF16), pltpu.VMEM((3, seq, LANES), BF16),
                        pltpu.VMEM((3, 2, seq, LANES), BF16), pltpu.VMEM((3, seq, LANES), F32),
                        pltpu.VMEM((2, seq, LANES), F32), pltpu.VMEM((2, seq, LANES), F32)],
        compiler_params=pltpu.CompilerParams(dimension_semantics=("arbitrary", "arbitrary"),
                                             vmem_limit_bytes=VMEM_LIMIT),
        name="dilated_attn",
    )(qa, ka, va, *consts)


FOX_T = 256
FOX_AHEAD = 3


def _fox_kernel(q_ref, k_ref, v_ref, aq_ref, ak_ref, o_ref, vaug):
    S = q_ref.shape[0]
    nq = S // FOX_T
    pair = pl.program_id(1)
    lane = lax.broadcasted_iota(jnp.int32, (FOX_T, LANES), 1)
    first_head = lane < HEAD_DIM
    row = lax.broadcasted_iota(jnp.int32, (FOX_T, FOX_T), 0)
    col = lax.broadcasted_iota(jnp.int32, (FOX_T, FOX_T), 1)
    causal = col <= row
    dn = (((1,), (1,)), ((), ()))

    fh = lax.broadcasted_iota(jnp.int32, (PREP_ROWS, LANES), 1) < HEAD_DIM
    one = jnp.ones((PREP_ROWS, LANES), BF16)
    for c0 in range(0, S, PREP_ROWS):
        v = v_ref[c0:c0 + PREP_ROWS, :]
        vaug[0, c0:c0 + PREP_ROWS, :] = jnp.where(fh, v, one)
        vaug[1, c0:c0 + PREP_ROWS, :] = jnp.where(fh, one, v)

    tasks = [(qi, hd) for qi in range(nq) for hd in range(2)]

    def scores(qi, hd):
        r0, n = qi * FOX_T, (qi + 1) * FOX_T
        q = q_ref[r0:r0 + FOX_T, :]
        aq = aq_ref[r0:r0 + FOX_T, :]
        lo = AUG * (2 * pair + hd)
        own = (lane >= lo) & (lane < lo + AUG)
        ql = jnp.concatenate(
            [jnp.where(first_head if hd == 0 else jnp.logical_not(first_head), q, jnp.zeros_like(q)),
             jnp.where(own, aq, jnp.zeros_like(aq))], axis=1)
        kk = jnp.concatenate([k_ref[0:n, :], ak_ref[0:n, :]], axis=1)
        s = lax.dot_general(ql, kk, dn, preferred_element_type=F32)
        tiles = [s[:, c:c + LANES] for c in range(0, n - FOX_T, LANES)]
        diag = jnp.where(causal, s[:, n - FOX_T:n], NEG)
        tiles += [diag[:, c:c + LANES] for c in range(0, FOX_T, LANES)]
        mx = tiles[0]
        for t in tiles[1:]:
            mx = jnp.maximum(mx, t)
        m = jnp.max(mx, axis=-1, keepdims=True)
        return jnp.concatenate([jnp.exp2(t - m).astype(BF16) for t in tiles], axis=1)

    pending = {}
    for i in range(len(tasks) + FOX_AHEAD):
        if i < len(tasks):
            pending[i] = scores(*tasks[i])
        j = i - FOX_AHEAD
        if j >= 0:
            qi, hd = tasks[j]
            n = (qi + 1) * FOX_T
            acc = jnp.dot(pending.pop(j), vaug[hd, 0:n, :], preferred_element_type=F32)
            if hd == 0:
                a0 = acc
            else:
                num = jnp.where(first_head, a0, acc)
                den = pltpu.roll(jnp.where(first_head, acc, a0), HEAD_DIM, 1)
                o_ref[qi * FOX_T:(qi + 1) * FOX_T, :] = (num / den).astype(o_ref.dtype)


def _fox_call(qb, kb, vb, aq, ak, *, batch, seq):
    npair = WIDTH_B // LANES
    spec = pl.BlockSpec((None, None, seq, LANES), lambda b, j: (b, j, 0, 0))
    aspec = pl.BlockSpec((None, seq, LANES), lambda b, j: (b, 0, 0))
    return pl.pallas_call(
        _fox_kernel,
        grid=(batch, npair),
        in_specs=[spec, spec, spec, aspec, aspec],
        out_specs=spec,
        out_shape=jax.ShapeDtypeStruct((batch, npair, seq, LANES), BF16),
        scratch_shapes=[pltpu.VMEM((2, seq, LANES), BF16)],
        compiler_params=pltpu.CompilerParams(dimension_semantics=("arbitrary", "arbitrary"),
                                             vmem_limit_bytes=VMEM_LIMIT),
        name="forget_attn",
    )(qb, kb, vb, aq.reshape(batch, seq, LANES), ak.reshape(batch, seq, LANES))


def _ffn_weights(w_gate, w_up, w_down):
    pad = D_FF_PAD - D_FF
    cols = lambda w: jnp.pad(w, ((0, 0), (0, pad))).astype(BF16)
    wd = jnp.pad(w_down, ((0, pad), (0, 0))).astype(BF16)
    return cols(w_gate), cols(w_up), wd


def kernel(x, c, positions, w_ada, b_ada, g_pre_ff1, g_post_ff1, w_ff1_gate, w_ff1_up, w_ff1_down,
           g_pre_mix, g_post_mix, w_in, b_forget, g_out_a, g_out_b, w_out,
           g_pre_ff2, g_post_ff2, w_ff2_gate, w_ff2_up, w_ff2_down):
    B, S, D = x.shape
    depth = w_ada.shape[0]
    tm = 512
    e_np, ec_np = _rope_consts()
    pq, pk, cq, ck = _forget_consts()
    tri = np.tril(np.ones((tm, tm), np.float32))
    consts = (jnp.asarray(e_np, BF16), jnp.asarray(ec_np, F32), jnp.asarray(tri, BF16),
              jnp.asarray(pq, BF16), jnp.asarray(pk, BF16), jnp.asarray(cq, F32), jnp.asarray(ck, F32))
    inv_freq = ROPE_THETA ** (-jnp.arange(0, ROT_DIM, 2, dtype=F32) / ROT_DIM)
    invf = inv_freq.reshape(ROT_DIM // 2, 1)
    pos3 = positions.reshape(B, 1, S)
    x2d = x.reshape(B * S, D)
    row = lambda g: g.reshape(1, -1)

    for l in range(depth):
        mod = _ada_call(c, w_ada[l], b_ada[l]).reshape(B, N_MOD, D)
        wg, wu, wd = _ffn_weights(w_ff1_gate[l], w_ff1_up[l], w_ff1_down[l])
        x2d = _ffn_call(x2d, mod, row(g_pre_ff1[l]), row(g_post_ff1[l]), wg, wu, wd, row0=0, seq=S, tm=2 * FFN_SUB)

        wqkv = w_in[l].astype(BF16)
        wf = jnp.pad(jnp.tile(w_in[l][:, QKV_COLS:], (1, 3)), ((0, 0), (0, LANES - 3 * N_HEADS_B))).astype(BF16)
        bfg = jnp.pad(jnp.tile(b_forget[l], 3), (0, LANES - 3 * N_HEADS_B)).reshape(1, LANES)
        qa, ka, va, qb, kb, vb, aq, ak = _proj_call(
            x2d, mod, row(g_pre_mix[l]), pos3, invf, wqkv, wf, bfg, consts, batch=B, seq=S, tm=tm)
        oa = _dil_call(qa, ka, va, batch=B, seq=S)
        ob = _fox_call(qb, kb, vb, aq, ak, batch=B, seq=S)
        wg, wu, wd = _ffn_weights(w_ff2_gate[l], w_ff2_up[l], w_ff2_down[l])
        mix = (oa, ob, row(g_out_a[l]), row(g_out_b[l]), w_out[l].astype(BF16), row(g_post_mix[l]))
        x2d = _ffn_call(x2d, mod, row(g_pre_ff2[l]), row(g_post_ff2[l]), wg, wu, wd, row0=6, seq=S,
                        tm=2 * FFN_SUB, mix=mix)
    return x2d.reshape(B, S, D)
```

```python
import functools
import math

import numpy as np
import jax
import jax.numpy as jnp
from jax import lax
from jax.experimental import pallas as pl
from jax.experimental.pallas import tpu as pltpu

F32 = jnp.float32
BF16 = jnp.bfloat16

D_MODEL = 1024
HEAD_DIM = 64
N_HEADS_A = 8
N_HEADS_B = 8
WIDTH_A = N_HEADS_A * HEAD_DIM
WIDTH_B = N_HEADS_B * HEAD_DIM
ROT_DIM = HEAD_DIM // 4
ROPE_THETA = 500000.0
D_FF = 2752
N_MOD = 9
EPS = 1e-6
ATTN_SCALE = HEAD_DIM ** -0.5
NEG = -1e30
QKV_COLS = 3 * WIDTH_A + 3 * WIDTH_B
LOG2E = math.log2(math.e)
Q_SCALE = ATTN_SCALE * LOG2E

LANES = 128
FF_CHUNK = 256
D_FF_PAD = -(-D_FF // FF_CHUNK) * FF_CHUNK
N_FF_CHUNKS = D_FF_PAD // FF_CHUNK
FFN_SUB = 512
PROJ_SUB = 512
BLK = 128
AUG = 16
VMEM_LIMIT = 56 * 1024 * 1024


def _rms(x, g):
    ms = jnp.mean(x * x, axis=-1, keepdims=True)
    return x * lax.rsqrt(ms + EPS) * g


def _split3(x):
    hi = x.astype(BF16).astype(F32)
    r1 = x - hi
    mid = r1.astype(BF16).astype(F32)
    lo = (r1 - mid).astype(BF16).astype(F32)
    return hi, mid, lo


def _const_spec(shape):
    nd = len(shape)
    return pl.BlockSpec(shape, lambda *_: (0,) * nd, pipeline_mode=pl.Buffered(1))


def _ada_kernel(c_ref, w_ref, b_ref, o_ref):
    c = c_ref[...]
    sc = (c * jax.nn.sigmoid(c)).astype(BF16)
    o_ref[...] = jnp.dot(sc, w_ref[...].astype(BF16), preferred_element_type=F32) + b_ref[...]


def _ada_call(c, w_ada, b_ada):
    B, D = c.shape
    N = w_ada.shape[1]
    tn = 1024
    return pl.pallas_call(
        _ada_kernel,
        grid=(N // tn,),
        in_specs=[pl.BlockSpec((B, D), lambda j: (0, 0)),
                  pl.BlockSpec((D, tn), lambda j: (0, j)),
                  pl.BlockSpec((1, tn), lambda j: (0, j))],
        out_specs=pl.BlockSpec((B, tn), lambda j: (0, j)),
        out_shape=jax.ShapeDtypeStruct((B, N), F32),
        compiler_params=pltpu.CompilerParams(dimension_semantics=("arbitrary",),
                                             vmem_limit_bytes=VMEM_LIMIT),
        name="adaln",
    )(c, w_ada, b_ada.reshape(1, N))


def _ffn_kernel(*refs, row0, with_mix):
    if with_mix:
        oa_ref, ob_ref, ga_ref, gb_ref, wo_ref, gmix_ref = refs[:6]
        refs = refs[6:]
    x_ref, mod_ref, gpre_ref, gpost_ref, wg_ref, wu_ref, wd_ref, o_ref, a_ref = refs
    shift = mod_ref[0, row0:row0 + 1, :]
    gs = gpre_ref[...] * (1.0 + mod_ref[0, row0 + 1:row0 + 2, :])
    gp = gpost_ref[...] * (0.5 * mod_ref[0, row0 + 2:row0 + 3, :])
    subs = [pl.ds(t * FFN_SUB, FFN_SUB) for t in range(x_ref.shape[0] // FFN_SUB)]
    xs = [x_ref[rows, :] for rows in subs]

    if with_mix:
        gm = gmix_ref[...] * mod_ref[0, 5:6, :]
        for t, rows in enumerate(subs):
            oa = jnp.concatenate([oa_ref[i, rows, :] for i in range(oa_ref.shape[0])], axis=1).astype(F32)
            ob = jnp.concatenate([ob_ref[i, rows, :] for i in range(ob_ref.shape[0])], axis=1).astype(F32)
            a = _rms(oa, ga_ref[...]).astype(BF16)
            b = _rms(ob, gb_ref[...]).astype(BF16)
            y = (jnp.dot(a, wo_ref[0:WIDTH_A, :], preferred_element_type=F32)
                 + jnp.dot(b, wo_ref[WIDTH_A:WIDTH_A + WIDTH_B, :], preferred_element_type=F32))
            xs[t] = xs[t] + _rms(y, gm)

    for t, rows in enumerate(subs):
        x = xs[t]
        ms = jnp.mean(x * x, axis=-1, keepdims=True)
        hb = (x * lax.rsqrt(ms + EPS) * gs + shift).astype(BF16)
        for c in range(N_FF_CHUNKS):
            cs = slice(c * FF_CHUNK, (c + 1) * FF_CHUNK)
            g = jnp.dot(hb, wg_ref[:, cs], preferred_element_type=F32)
            u = jnp.dot(hb, wu_ref[:, cs], preferred_element_type=F32)
            a_ref[t, :, cs] = (g * jax.nn.sigmoid(g) * u).astype(BF16)
        y = jnp.dot(a_ref[t], wd_ref[...], preferred_element_type=F32)
        o_ref[rows, :] = x + _rms(y, gp)


def _ffn_call(x2d, mod, g_pre, g_post, wg, wu, wd, *, row0, seq, tm, mix=None):
    T, D = x2d.shape
    per_seq = seq // tm
    kern = functools.partial(_ffn_kernel, row0=row0, with_mix=mix is not None)
    args, specs = [], []
    if mix is not None:
        oa, ob, ga, gb, wo, gmix = mix
        pair_spec = lambda t: pl.BlockSpec((None, t.shape[1], tm, LANES),
                                           lambda i: (i // per_seq, 0, i % per_seq, 0))
        args += [oa, ob, ga, gb, wo, gmix]
        specs += [pair_spec(oa), pair_spec(ob), _const_spec(ga.shape), _const_spec(gb.shape),
                  _const_spec(wo.shape), _const_spec(gmix.shape)]
    args += [x2d, mod, g_pre, g_post, wg, wu, wd]
    specs += [pl.BlockSpec((tm, D), lambda i: (i, 0)),
              pl.BlockSpec((1, N_MOD, D), lambda i: (i // per_seq, 0, 0)),
              _const_spec((1, D)), _const_spec((1, D)),
              _const_spec(wg.shape), _const_spec(wu.shape), _const_spec(wd.shape)]
    return pl.pallas_call(
        kern,
        grid=(T // tm,),
        in_specs=specs,
        out_specs=pl.BlockSpec((tm, D), lambda i: (i, 0)),
        out_shape=jax.ShapeDtypeStruct((T, D), F32),
        scratch_shapes=[pltpu.VMEM((tm // FFN_SUB, FFN_SUB, D_FF_PAD), BF16)],
        compiler_params=pltpu.CompilerParams(dimension_semantics=("arbitrary",),
                                             vmem_limit_bytes=VMEM_LIMIT),
        name="ffn_row%d" % row0,
    )(*args)


def _rope_consts():
    half = ROT_DIM // 2
    e = np.zeros((3, 2 * half, 3 * LANES), np.float32)
    c = np.zeros((1, 3 * LANES), np.float32)
    for lane in range(LANES):
        d = lane % HEAD_DIM
        if d < ROT_DIM:
            e[:, d % half, lane] = 1.0
        else:
            c[0, lane] = 1.0
        if d < half:
            e[:, half + d, LANES + lane] = -1.0
        elif d < ROT_DIM:
            e[:, half + d - half, 2 * LANES + lane] = 1.0
    return e.reshape(3 * 2 * half, 3 * LANES), c


def _forget_consts():
    pq = np.zeros((LANES, LANES), np.float32)
    pk = np.zeros((LANES, LANES), np.float32)
    cq = np.zeros((1, LANES), np.float32)
    ck = np.zeros((1, LANES), np.float32)
    for h in range(N_HEADS_B):
        for g in range(3):
            pq[8 * g + h, AUG * h + g] = 1.0
            cq[0, AUG * h + 3 + g] = 1.0
            pk[8 * g + h, AUG * h + 3 + g] = -1.0
            ck[0, AUG * h + g] = 1.0
    return pq, pk, cq, ck


def _proj_kernel(x_ref, mod_ref, g_ref, pos_ref, invf_ref, wqkv_ref, wf_ref, bf_ref, e_ref, ec_ref, tri_ref,
                 pq_ref, pk_ref, cq_ref, ck_ref,
                 qa_ref, ka_ref, va_ref, qb_ref, kb_ref, vb_ref, aq_ref, ak_ref, carry_ref):
    @pl.when(pl.program_id(1) == 0)
    def _():
        carry_ref[...] = jnp.zeros_like(carry_ref)

    shift = mod_ref[0, 3:4, :]
    gs = g_ref[...] * (1.0 + mod_ref[0, 4:5, :])
    half = ROT_DIM // 2
    carry = carry_ref[...]
    for t in range(x_ref.shape[0] // PROJ_SUB):
        rows = pl.ds(t * PROJ_SUB, PROJ_SUB)
        x = x_ref[rows, :]
        ms = jnp.mean(x * x, axis=-1, keepdims=True)
        hb = (x * lax.rsqrt(ms + EPS) * gs + shift).astype(BF16)

        def cols(c0, width, hb=hb):
            return jnp.dot(hb, wqkv_ref[:, c0:c0 + width], preferred_element_type=F32)

        def put_pairs(ref, u, fn=lambda v: v, rows=rows):
            for i in range(u.shape[1] // LANES):
                ref[i, rows, :] = fn(u[:, i * LANES:(i + 1) * LANES]).astype(ref.dtype)


        fl = jnp.dot(hb, wf_ref[...], preferred_element_type=F32) + bf_ref[...]
        put_pairs(va_ref, cols(2 * WIDTH_A, WIDTH_A))

        posf = pos_ref[:, t * PROJ_SUB:(t + 1) * PROJ_SUB].astype(F32)
        ang = invf_ref[...] * posf
        cs = jnp.concatenate([jnp.cos(ang), jnp.sin(ang)], axis=0)
        cs3 = jnp.concatenate([u.astype(BF16) for u in _split3(cs)], axis=0)
        tabs = lax.dot_general(cs3, e_ref[...], (((0,), (0,)), ((), ())),
                               preferred_element_type=F32) + ec_ref[...]
        t_cos = tabs[:, 0:LANES]
        t_s1 = tabs[:, LANES:2 * LANES]
        t_s2 = tabs[:, 2 * LANES:3 * LANES]

        def rope(u, t_cos=t_cos, t_s1=t_s1, t_s2=t_s2):
            return u * t_cos + pltpu.roll(u, LANES - half, 1) * t_s1 + pltpu.roll(u, half, 1) * t_s2

        lf = jnp.minimum(fl, 0.0) - jnp.log1p(jnp.exp(-jnp.abs(fl)))
        lane = lax.broadcasted_iota(jnp.int32, lf.shape, 1)
        zero = jnp.zeros_like(lf)

        def by_group(a, b, c, lane=lane, zero=zero):
            return jnp.where(lane < 8, a, jnp.where(lane < 16, b, jnp.where(lane < 24, c, zero))).astype(BF16)

        cum = jnp.dot(tri_ref[...], by_group(*_split3(lf)), preferred_element_type=F32)

        put_pairs(qa_ref, cols(0, WIDTH_A), lambda u: rope(u) * Q_SCALE)

        cum = cum + pltpu.roll(cum, LANES - 8, 1) + pltpu.roll(cum, LANES - 16, 1)
        f_tot = cum + carry
        carry = f_tot[PROJ_SUB - 1:PROJ_SUB, :]
        p1, p2, p3 = _split3(f_tot * LOG2E)
        pieces = by_group(p1, pltpu.roll(p2, 8, 1), pltpu.roll(p3, 16, 1))
        aq_ref[rows, :] = (jnp.dot(pieces, pq_ref[...], preferred_element_type=F32) + cq_ref[...]).astype(BF16)
        ak_ref[rows, :] = (jnp.dot(pieces, pk_ref[...], preferred_element_type=F32) + ck_ref[...]).astype(BF16)

        put_pairs(ka_ref, cols(WIDTH_A, WIDTH_A), rope)
        o = 3 * WIDTH_A
        put_pairs(qb_ref, cols(o, WIDTH_B), lambda u: u * Q_SCALE)
        put_pairs(kb_ref, cols(o + WIDTH_B, WIDTH_B))
        put_pairs(vb_ref, cols(o + 2 * WIDTH_B, WIDTH_B))
    carry_ref[...] = carry


def _proj_call(x2d, mod, g_pre, pos3, invf, wqkv, wf, bfg, consts, *, batch, seq, tm):
    T, D = x2d.shape
    per_seq = seq // tm
    tok = lambda b, j: (b * per_seq + j, 0)
    npa, npb = WIDTH_A // LANES, WIDTH_B // LANES
    pairs = lambda b, j: (b, 0, j, 0)
    out_shape = ([jax.ShapeDtypeStruct((batch, npa, seq, LANES), F32)] * 3
                 + [jax.ShapeDtypeStruct((batch, npb, seq, LANES), BF16)] * 3
                 + [jax.ShapeDtypeStruct((T, LANES), BF16)] * 2)
    out_specs = ([pl.BlockSpec((None, npa, tm, LANES), pairs)] * 3
                 + [pl.BlockSpec((None, npb, tm, LANES), pairs)] * 3
                 + [pl.BlockSpec((tm, LANES), tok)] * 2)
    return pl.pallas_call(
        _proj_kernel,
        grid=(batch, per_seq),
        in_specs=[pl.BlockSpec((tm, D), tok),
                  pl.BlockSpec((1, N_MOD, D), lambda b, j: (b, 0, 0)),
                  _const_spec((1, D)),
                  pl.BlockSpec((None, 1, tm), lambda b, j: (b, 0, j)),
                  _const_spec(invf.shape), _const_spec(wqkv.shape), _const_spec(wf.shape),
                  _const_spec(bfg.shape)] + [_const_spec(t.shape) for t in consts],
        out_specs=out_specs,
        out_shape=out_shape,
        scratch_shapes=[pltpu.VMEM((1, LANES), F32)],
        compiler_params=pltpu.CompilerParams(dimension_semantics=("arbitrary", "arbitrary"),
                                             vmem_limit_bytes=VMEM_LIMIT),
        name="mixer_proj",
    )(x2d, mod, g_pre, pos3, invf, wqkv, wf, bfg, *consts)


DIL_AHEAD = 6
PREP_ROWS = 512


def _dil_consts(seq):
    a = np.arange(BLK)[:, None]
    j = np.arange(2 * BLK)[None, :]
    band_ok = np.where(j < BLK, j >= a, j - BLK <= a)
    causal_ok = np.arange(BLK)[None, :] <= a
    n4 = seq // 4
    dist = (n4 - BLK) + a - np.arange(n4)[None, :]
    count = ((dist >= 0) & (dist <= BLK)).astype(np.int32) + ((dist >= 0) & (dist % 4 == 0)).astype(np.int32)
    joint = np.where(count == 0, NEG, np.log2(np.maximum(count, 1)))
    bias = lambda ok: np.where(ok, 0.0, NEG)
    return [t.astype(np.float32) for t in (bias(band_ok), bias(causal_ok), joint)]


def _dil_kernel(q_ref, k_ref, v_ref, band_ref, causal_ref, joint_ref, o_ref, qp, kp, vp, o_sc, l_sc):
    S = q_ref.shape[0]
    nblk = S // BLK
    n4 = S // 4
    first_head = lax.broadcasted_iota(jnp.int32, (BLK, LANES), 1) < HEAD_DIM
    dn = (((1,), (1,)), ((), ()))

    for ref, dst in ((q_ref, qp), (k_ref, kp), (v_ref, vp)):
        for r in range(4):
            dst[1, pl.ds(r * n4, n4), :] = ref[pl.ds(r, n4, stride=4), :].astype(BF16)
        for c0 in range(0, S, PREP_ROWS):
            dst[0, pl.ds(c0, PREP_ROWS), :] = ref[pl.ds(c0, PREP_ROWS), :].astype(BF16)

    tasks = [(p, ib) for p in (1, 0) for ib in range(nblk)]

    def keys_of(p, ib):
        if p == 1:
            cls, jb = divmod(ib, n4 // BLK)
            n = (jb + 1) * BLK
            return pl.ds(cls * n4, n), joint_ref[:, n4 - n:n4]
        if ib == 0:
            return pl.ds(0, BLK), causal_ref[...]
        return pl.ds((ib - 1) * BLK, 2 * BLK), band_ref[...]

    def scores(p, ib):
        q = qp[p, pl.ds(ib * BLK, BLK), :]
        zero = jnp.zeros_like(q)
        q2 = jnp.concatenate([jnp.where(first_head, q, zero), jnp.where(first_head, zero, q)], axis=0)
        keys, bias = keys_of(p, ib)
        s = lax.dot_general(q2, kp[p, keys, :], dn, preferred_element_type=F32)
        ms, ls, es = [], [], []
        for hd in range(2):
            sh = s[hd * BLK:(hd + 1) * BLK, :] + bias
            m = jnp.max(sh, axis=-1, keepdims=True)
            e = jnp.exp2(sh - m)
            ms.append(m)
            ls.append(jnp.sum(e, axis=-1, keepdims=True))
            es.append(e.astype(BF16))
        return ms, ls, jnp.concatenate(es, axis=0)

    def finish(p, ib, ms, ls, acc):
        num = jnp.where(first_head, acc[0:BLK, :], acc[BLK:2 * BLK, :])
        den = jnp.where(first_head, ls[0], ls[1])
        o = num / den
        lse = jnp.where(first_head, ms[0], ms[1]) + jnp.log2(den)
        if p == 1:
            cls, t0 = divmod(ib * BLK, n4)
            dst = pl.ds(cls + t0 * 4, BLK, stride=4)
            o_sc[dst, :] = o
            l_sc[dst, :] = lse
        else:
            rows = pl.ds(ib * BLK, BLK)
            o2, lse2 = o_sc[rows, :], l_sc[rows, :]
            mx = jnp.maximum(lse, lse2)
            w, w2 = jnp.exp2(lse - mx), jnp.exp2(lse2 - mx)
            o_ref[rows, :] = ((w * o + w2 * o2) / (w + w2)).astype(o_ref.dtype)

    pending = {}
    for i in range(len(tasks) + DIL_AHEAD):
        if i < len(tasks):
            pending[i] = scores(*tasks[i])
        j = i - DIL_AHEAD
        if j >= 0:
            p, ib = tasks[j]
            ms, ls, e2 = pending.pop(j)
            acc = jnp.dot(e2, vp[p, keys_of(p, ib)[0], :], preferred_element_type=F32)
            finish(p, ib, ms, ls, acc)


def _dil_call(qa, ka, va, *, batch, seq):
    npair = WIDTH_A // LANES
    spec = pl.BlockSpec((None, None, seq, LANES), lambda b, j: (b, j, 0, 0))
    consts = [jnp.asarray(t, F32) for t in _dil_consts(seq)]
    return pl.pallas_call(
        _dil_kernel,
        grid=(batch, npair),
        in_specs=[spec, spec, spec] + [_const_spec(t.shape) for t in consts],
        out_specs=spec,
        out_shape=jax.ShapeDtypeStruct((batch, npair, seq, LANES), BF16),
        scratch_shapes=[pltpu.VMEM((2, seq, LANES), BF16), pltpu.VMEM((2, seq, LANES), BF16),
                        pltpu.VMEM((2, seq, LANES), BF16),
                        pltpu.VMEM((seq, LANES), F32), pltpu.VMEM((seq, LANES), F32)],
        compiler_params=pltpu.CompilerParams(dimension_semantics=("arbitrary", "arbitrary"),
                                             vmem_limit_bytes=VMEM_LIMIT),
        name="dilated_attn",
    )(qa, ka, va, *consts)


FOX_T = 256
FOX_AHEAD = 3


def _fox_kernel(q_ref, k_ref, v_ref, aq_ref, ak_ref, o_ref, vaug):
    S = q_ref.shape[0]
    nq = S // FOX_T
    pair = pl.program_id(1)
    lane = lax.broadcasted_iota(jnp.int32, (FOX_T, LANES), 1)
    first_head = lane < HEAD_DIM
    row = lax.broadcasted_iota(jnp.int32, (FOX_T, FOX_T), 0)
    col = lax.broadcasted_iota(jnp.int32, (FOX_T, FOX_T), 1)
    causal = col <= row
    dn = (((1,), (1,)), ((), ()))

    fh = lax.broadcasted_iota(jnp.int32, (PREP_ROWS, LANES), 1) < HEAD_DIM
    one = jnp.ones((PREP_ROWS, LANES), BF16)
    for c0 in range(0, S, PREP_ROWS):
        v = v_ref[c0:c0 + PREP_ROWS, :]
        vaug[0, c0:c0 + PREP_ROWS, :] = jnp.where(fh, v, one)
        vaug[1, c0:c0 + PREP_ROWS, :] = jnp.where(fh, one, v)

    tasks = [(qi, hd) for qi in range(nq) for hd in range(2)]

    def scores(qi, hd):
        r0, n = qi * FOX_T, (qi + 1) * FOX_T
        q = q_ref[r0:r0 + FOX_T, :]
        aq = aq_ref[r0:r0 + FOX_T, :]
        lo = AUG * (2 * pair + hd)
        own = (lane >= lo) & (lane < lo + AUG)
        ql = jnp.concatenate(
            [jnp.where(first_head if hd == 0 else jnp.logical_not(first_head), q, jnp.zeros_like(q)),
             jnp.where(own, aq, jnp.zeros_like(aq))], axis=1)
        kk = jnp.concatenate([k_ref[0:n, :], ak_ref[0:n, :]], axis=1)
        s = lax.dot_general(ql, kk, dn, preferred_element_type=F32)
        tiles = [s[:, c:c + LANES] for c in range(0, n - FOX_T, LANES)]
        diag = jnp.where(causal, s[:, n - FOX_T:n], NEG)
        tiles += [diag[:, c:c + LANES] for c in range(0, FOX_T, LANES)]
        mx = tiles[0]
        for t in tiles[1:]:
            mx = jnp.maximum(mx, t)
        m = jnp.max(mx, axis=-1, keepdims=True)
        return jnp.concatenate([jnp.exp2(t - m).astype(BF16) for t in tiles], axis=1)

    pending = {}
    for i in range(len(tasks) + FOX_AHEAD):
        if i < len(tasks):
            pending[i] = scores(*tasks[i])
        j = i - FOX_AHEAD
        if j >= 0:
            qi, hd = tasks[j]
            n = (qi + 1) * FOX_T
            acc = jnp.dot(pending.pop(j), vaug[hd, 0:n, :], preferred_element_type=F32)
            if hd == 0:
                a0 = acc
            else:
                num = jnp.where(first_head, a0, acc)
                den = pltpu.roll(jnp.where(first_head, acc, a0), HEAD_DIM, 1)
                o_ref[qi * FOX_T:(qi + 1) * FOX_T, :] = (num / den).astype(o_ref.dtype)


def _fox_call(qb, kb, vb, aq, ak, *, batch, seq):
    npair = WIDTH_B // LANES
    spec = pl.BlockSpec((None, None, seq, LANES), lambda b, j: (b, j, 0, 0))
    aspec = pl.BlockSpec((None, seq, LANES), lambda b, j: (b, 0, 0))
    return pl.pallas_call(
        _fox_kernel,
        grid=(batch, npair),
        in_specs=[spec, spec, spec, aspec, aspec],
        out_specs=spec,
        out_shape=jax.ShapeDtypeStruct((batch, npair, seq, LANES), BF16),
        scratch_shapes=[pltpu.VMEM((2, seq, LANES), BF16)],
        compiler_params=pltpu.CompilerParams(dimension_semantics=("arbitrary", "arbitrary"),
                                             vmem_limit_bytes=VMEM_LIMIT),
        name="forget_attn",
    )(qb, kb, vb, aq.reshape(batch, seq, LANES), ak.reshape(batch, seq, LANES))


def _ffn_weights(w_gate, w_up, w_down):
    pad = D_FF_PAD - D_FF
    cols = lambda w: jnp.pad(w.astype(BF16), ((0, 0), (0, pad)))
    wd = jnp.pad(w_down.astype(BF16), ((0, pad), (0, 0)))
    return cols(w_gate), cols(w_up), wd


def kernel(x, c, positions, w_ada, b_ada, g_pre_ff1, g_post_ff1, w_ff1_gate, w_ff1_up, w_ff1_down,
           g_pre_mix, g_post_mix, w_in, b_forget, g_out_a, g_out_b, w_out,
           g_pre_ff2, g_post_ff2, w_ff2_gate, w_ff2_up, w_ff2_down):
    B, S, D = x.shape
    depth = w_ada.shape[0]
    tm = 2 * PROJ_SUB
    e_np, ec_np = _rope_consts()
    pq, pk, cq, ck = _forget_consts()
    tri = np.tril(np.ones((PROJ_SUB, PROJ_SUB), np.float32))
    consts = (jnp.asarray(e_np, BF16), jnp.asarray(ec_np, F32), jnp.asarray(tri, BF16),
              jnp.asarray(pq, BF16), jnp.asarray(pk, BF16), jnp.asarray(cq, F32), jnp.asarray(ck, F32))
    inv_freq = ROPE_THETA ** (-jnp.arange(0, ROT_DIM, 2, dtype=F32) / ROT_DIM)
    invf = inv_freq.reshape(ROT_DIM // 2, 1)
    pos3 = positions.reshape(B, 1, S)
    x2d = x.reshape(B * S, D)
    row = lambda g: g.reshape(1, -1)

    for l in range(depth):
        mod = _ada_call(c, w_ada[l], b_ada[l]).reshape(B, N_MOD, D)
        wg, wu, wd = _ffn_weights(w_ff1_gate[l], w_ff1_up[l], w_ff1_down[l])
        x2d = _ffn_call(x2d, mod, row(g_pre_ff1[l]), row(g_post_ff1[l]), wg, wu, wd, row0=0, seq=S, tm=2 * FFN_SUB)

        wqkv = w_in[l].astype(BF16)
        wf = jnp.pad(jnp.tile(w_in[l][:, QKV_COLS:], (1, 3)), ((0, 0), (0, LANES - 3 * N_HEADS_B))).astype(BF16)
        bfg = jnp.pad(jnp.tile(b_forget[l], 3), (0, LANES - 3 * N_HEADS_B)).reshape(1, LANES)
        qa, ka, va, qb, kb, vb, aq, ak = _proj_call(
            x2d, mod, row(g_pre_mix[l]), pos3, invf, wqkv, wf, bfg, consts, batch=B, seq=S, tm=tm)
        oa = _dil_call(qa, ka, va, batch=B, seq=S)
        ob = _fox_call(qb, kb, vb, aq, ak, batch=B, seq=S)
        wg, wu, wd = _ffn_weights(w_ff2_gate[l], w_ff2_up[l], w_ff2_down[l])
        mix = (oa, ob, row(g_out_a[l]), row(g_out_b[l]), w_out[l].astype(BF16), row(g_post_mix[l]))
        x2d = _ffn_call(x2d, mod, row(g_pre_ff2[l]), row(g_post_ff2[l]), wg, wu, wd, row0=6, seq=S,
                        tm=2 * FFN_SUB, mix=mix)
    return x2d.reshape(B, S, D)
```

```python
import functools
import math

import numpy as np
import jax
import jax.numpy as jnp
from jax import lax
from jax.experimental import pallas as pl
from jax.experimental.pallas import tpu as pltpu

F32 = jnp.float32
BF16 = jnp.bfloat16

D_MODEL = 1024
HEAD_DIM = 64
N_HEADS_A = 8
N_HEADS_B = 8
WIDTH_A = N_HEADS_A * HEAD_DIM
WIDTH_B = N_HEADS_B * HEAD_DIM
ROT_DIM = HEAD_DIM // 4
ROPE_THETA = 500000.0
D_FF = 2752
N_MOD = 9
EPS = 1e-6
ATTN_SCALE = HEAD_DIM ** -0.5
NEG = -1e30
QKV_COLS = 3 * WIDTH_A + 3 * WIDTH_B
LOG2E = math.log2(math.e)
Q_SCALE = ATTN_SCALE * LOG2E

LANES = 128
FF_CHUNK = 256
D_FF_PAD = -(-D_FF // FF_CHUNK) * FF_CHUNK
N_FF_CHUNKS = D_FF_PAD // FF_CHUNK
FFN_SUB = 512
PROJ_SUB = 512
BLK = 128
AUG = 16
VMEM_LIMIT = 56 * 1024 * 1024


def _rms(x, g):
    ms = jnp.mean(x * x, axis=-1, keepdims=True)
    return x * lax.rsqrt(ms + EPS) * g


def _split3(x):
    hi = x.astype(BF16).astype(F32)
    r1 = x - hi
    mid = r1.astype(BF16).astype(F32)
    lo = (r1 - mid).astype(BF16).astype(F32)
    return hi, mid, lo


def _const_spec(shape):
    nd = len(shape)
    return pl.BlockSpec(shape, lambda *_: (0,) * nd, pipeline_mode=pl.Buffered(1))


def _ada_kernel(c_ref, w_ref, b_ref, o_ref):
    c = c_ref[...]
    sc = (c * jax.nn.sigmoid(c)).astype(BF16)
    o_ref[...] = jnp.dot(sc, w_ref[...].astype(BF16), preferred_element_type=F32) + b_ref[...]


def _ada_call(c, w_ada, b_ada):
    B, D = c.shape
    N = w_ada.shape[1]
    tn = 1024
    return pl.pallas_call(
        _ada_kernel,
        grid=(N // tn,),
        in_specs=[pl.BlockSpec((B, D), lambda j: (0, 0)),
                  pl.BlockSpec((D, tn), lambda j: (0, j)),
                  pl.BlockSpec((1, tn), lambda j: (0, j))],
        out_specs=pl.BlockSpec((B, tn), lambda j: (0, j)),
        out_shape=jax.ShapeDtypeStruct((B, N), F32),
        compiler_params=pltpu.CompilerParams(dimension_semantics=("arbitrary",),
                                             vmem_limit_bytes=VMEM_LIMIT),
        name="adaln",
    )(c, w_ada, b_ada.reshape(1, N))


def _ffn_kernel(*refs, row0, with_mix):
    if with_mix:
        oa_ref, ob_ref, ga_ref, gb_ref, wo_ref, gmix_ref = refs[:6]
        refs = refs[6:]
    x_ref, mod_ref, gpre_ref, gpost_ref, wg_ref, wu_ref, wd_ref, o_ref, a_ref = refs
    shift = mod_ref[0, row0:row0 + 1, :]
    gs = gpre_ref[...] * (1.0 + mod_ref[0, row0 + 1:row0 + 2, :])
    gp = gpost_ref[...] * (0.5 * mod_ref[0, row0 + 2:row0 + 3, :])
    subs = [pl.ds(t * FFN_SUB, FFN_SUB) for t in range(x_ref.shape[0] // FFN_SUB)]
    xs = [x_ref[rows, :] for rows in subs]

    if with_mix:
        gm = gmix_ref[...] * mod_ref[0, 5:6, :]
        for t, rows in enumerate(subs):
            oa = jnp.concatenate([oa_ref[i, rows, :] for i in range(oa_ref.shape[0])], axis=1).astype(F32)
            ob = jnp.concatenate([ob_ref[i, rows, :] for i in range(ob_ref.shape[0])], axis=1).astype(F32)
            a = _rms(oa, ga_ref[...]).astype(BF16)
            b = _rms(ob, gb_ref[...]).astype(BF16)
            y = (jnp.dot(a, wo_ref[0:WIDTH_A, :], preferred_element_type=F32)
                 + jnp.dot(b, wo_ref[WIDTH_A:WIDTH_A + WIDTH_B, :], preferred_element_type=F32))
            xs[t] = xs[t] + _rms(y, gm)

    for t, rows in enumerate(subs):
        x = xs[t]
        ms = jnp.mean(x * x, axis=-1, keepdims=True)
        hb = (x * lax.rsqrt(ms + EPS) * gs + shift).astype(BF16)
        for c in range(N_FF_CHUNKS):
            cs = slice(c * FF_CHUNK, (c + 1) * FF_CHUNK)
            g = jnp.dot(hb, wg_ref[:, cs], preferred_element_type=F32)
            u = jnp.dot(hb, wu_ref[:, cs], preferred_element_type=F32)
            a_ref[t, :, cs] = (g * jax.nn.sigmoid(g) * u).astype(BF16)
        y = jnp.dot(a_ref[t], wd_ref[...], preferred_element_type=F32)
        o_ref[rows, :] = x + _rms(y, gp)


def _ffn_call(x2d, mod, g_pre, g_post, wg, wu, wd, *, row0, seq, tm, mix=None):
    T, D = x2d.shape
    per_seq = seq // tm
    kern = functools.partial(_ffn_kernel, row0=row0, with_mix=mix is not None)
    args, specs = [], []
    if mix is not None:
        oa, ob, ga, gb, wo, gmix = mix
        pair_spec = lambda t: pl.BlockSpec((None, t.shape[1], tm, LANES),
                                           lambda i: (i // per_seq, 0, i % per_seq, 0))
        args += [oa, ob, ga, gb, wo, gmix]
        specs += [pair_spec(oa), pair_spec(ob), _const_spec(ga.shape), _const_spec(gb.shape),
                  _const_spec(wo.shape), _const_spec(gmix.shape)]
    args += [x2d, mod, g_pre, g_post, wg, wu, wd]
    specs += [pl.BlockSpec((tm, D), lambda i: (i, 0)),
              pl.BlockSpec((1, N_MOD, D), lambda i: (i // per_seq, 0, 0)),
              _const_spec((1, D)), _const_spec((1, D)),
              _const_spec(wg.shape), _const_spec(wu.shape), _const_spec(wd.shape)]
    return pl.pallas_call(
        kern,
        grid=(T // tm,),
        in_specs=specs,
        out_specs=pl.BlockSpec((tm, D), lambda i: (i, 0)),
        out_shape=jax.ShapeDtypeStruct((T, D), F32),
        scratch_shapes=[pltpu.VMEM((tm // FFN_SUB, FFN_SUB, D_FF_PAD), BF16)],
        compiler_params=pltpu.CompilerParams(dimension_semantics=("arbitrary",),
                                             vmem_limit_bytes=VMEM_LIMIT),
        name="ffn_row%d" % row0,
    )(*args)


def _rope_consts():
    half = ROT_DIM // 2
    e = np.zeros((3, 2 * half, 3 * LANES), np.float32)
    c = np.zeros((1, 3 * LANES), np.float32)
    for lane in range(LANES):
        d = lane % HEAD_DIM
        if d < ROT_DIM:
            e[:, d % half, lane] = 1.0
        else:
            c[0, lane] = 1.0
        if d < half:
            e[:, half + d, LANES + lane] = -1.0
        elif d < ROT_DIM:
            e[:, half + d - half, 2 * LANES + lane] = 1.0
    return e.reshape(3 * 2 * half, 3 * LANES), c


def _forget_consts():
    pq = np.zeros((LANES, LANES), np.float32)
    pk = np.zeros((LANES, LANES), np.float32)
    cq = np.zeros((1, LANES), np.float32)
    ck = np.zeros((1, LANES), np.float32)
    for h in range(N_HEADS_B):
        for g in range(3):
            pq[8 * g + h, AUG * h + g] = 1.0
            cq[0, AUG * h + 3 + g] = 1.0
            pk[8 * g + h, AUG * h + 3 + g] = -1.0
            ck[0, AUG * h + g] = 1.0
    return pq, pk, cq, ck


def _proj_kernel(x_ref, mod_ref, g_ref, pos_ref, invf_ref, wqkv_ref, wf_ref, bf_ref, e_ref, ec_ref, tri_ref,
                 pq_ref, pk_ref, cq_ref, ck_ref,
                 qa_ref, ka_ref, va_ref, qb_ref, kb_ref, vb_ref, aq_ref, ak_ref, carry_ref):
    @pl.when(pl.program_id(1) == 0)
    def _():
        carry_ref[...] = jnp.zeros_like(carry_ref)

    shift = mod_ref[0, 3:4, :]
    gs = g_ref[...] * (1.0 + mod_ref[0, 4:5, :])
    half = ROT_DIM // 2
    carry = carry_ref[...]
    for t in range(x_ref.shape[0] // PROJ_SUB):
        rows = pl.ds(t * PROJ_SUB, PROJ_SUB)
        x = x_ref[rows, :]
        ms = jnp.mean(x * x, axis=-1, keepdims=True)
        hb = (x * lax.rsqrt(ms + EPS) * gs + shift).astype(BF16)

        def cols(c0, width, hb=hb):
            return jnp.dot(hb, wqkv_ref[:, c0:c0 + width], preferred_element_type=F32)

        def put_pairs(ref, u, fn=lambda v: v, rows=rows):
            for i in range(u.shape[1] // LANES):
                ref[i, rows, :] = fn(u[:, i * LANES:(i + 1) * LANES]).astype(ref.dtype)


        fl = jnp.dot(hb, wf_ref[...], preferred_element_type=F32) + bf_ref[...]
        put_pairs(va_ref, cols(2 * WIDTH_A, WIDTH_A))

        posf = pos_ref[:, t * PROJ_SUB:(t + 1) * PROJ_SUB].astype(F32)
        ang = invf_ref[...] * posf
        cs = jnp.concatenate([jnp.cos(ang), jnp.sin(ang)], axis=0)
        cs3 = jnp.concatenate([u.astype(BF16) for u in _split3(cs)], axis=0)
        tabs = lax.dot_general(cs3, e_ref[...], (((0,), (0,)), ((), ())),
                               preferred_element_type=F32) + ec_ref[...]
        t_cos = tabs[:, 0:LANES]
        t_s1 = tabs[:, LANES:2 * LANES]
        t_s2 = tabs[:, 2 * LANES:3 * LANES]

        def rope(u, t_cos=t_cos, t_s1=t_s1, t_s2=t_s2):
            return u * t_cos + pltpu.roll(u, LANES - half, 1) * t_s1 + pltpu.roll(u, half, 1) * t_s2

        lf = jnp.minimum(fl, 0.0) - jnp.log1p(jnp.exp(-jnp.abs(fl)))
        lane = lax.broadcasted_iota(jnp.int32, lf.shape, 1)
        zero = jnp.zeros_like(lf)

        def by_group(a, b, c, lane=lane, zero=zero):
            return jnp.where(lane < 8, a, jnp.where(lane < 16, b, jnp.where(lane < 24, c, zero))).astype(BF16)

        cum = jnp.dot(tri_ref[...], by_group(*_split3(lf)), preferred_element_type=F32)

        put_pairs(qa_ref, cols(0, WIDTH_A), lambda u: rope(u) * Q_SCALE)

        cum = cum + pltpu.roll(cum, LANES - 8, 1) + pltpu.roll(cum, LANES - 16, 1)
        f_tot = cum + carry
        carry = f_tot[PROJ_SUB - 1:PROJ_SUB, :]
        p1, p2, p3 = _split3(f_tot * LOG2E)
        pieces = by_group(p1, pltpu.roll(p2, 8, 1), pltpu.roll(p3, 16, 1))
        aq_ref[rows, :] = (jnp.dot(pieces, pq_ref[...], preferred_element_type=F32) + cq_ref[...]).astype(BF16)
        ak_ref[rows, :] = (jnp.dot(pieces, pk_ref[...], preferred_element_type=F32) + ck_ref[...]).astype(BF16)

        put_pairs(ka_ref, cols(WIDTH_A, WIDTH_A), rope)
        o = 3 * WIDTH_A
        put_pairs(qb_ref, cols(o, WIDTH_B), lambda u: u * Q_SCALE)
        put_pairs(kb_ref, cols(o + WIDTH_B, WIDTH_B))
        put_pairs(vb_ref, cols(o + 2 * WIDTH_B, WIDTH_B))
    carry_ref[...] = carry


def _proj_call(x2d, mod, g_pre, pos3, invf, wqkv, wf, bfg, consts, *, batch, seq, tm):
    T, D = x2d.shape
    per_seq = seq // tm
    tok = lambda b, j: (b * per_seq + j, 0)
    npa, npb = WIDTH_A // LANES, WIDTH_B // LANES
    pairs = lambda b, j: (b, 0, j, 0)
    out_shape = ([jax.ShapeDtypeStruct((batch, npa, seq, LANES), F32)] * 3
                 + [jax.ShapeDtypeStruct((batch, npb, seq, LANES), BF16)] * 3
                 + [jax.ShapeDtypeStruct((T, LANES), BF16)] * 2)
    out_specs = ([pl.BlockSpec((None, npa, tm, LANES), pairs)] * 3
                 + [pl.BlockSpec((None, npb, tm, LANES), pairs)] * 3
                 + [pl.BlockSpec((tm, LANES), tok)] * 2)
    return pl.pallas_call(
        _proj_kernel,
        grid=(batch, per_seq),
        in_specs=[pl.BlockSpec((tm, D), tok),
                  pl.BlockSpec((1, N_MOD, D), lambda b, j: (b, 0, 0)),
                  _const_spec((1, D)),
                  pl.BlockSpec((None, 1, tm), lambda b, j: (b, 0, j)),
                  _const_spec(invf.shape), _const_spec(wqkv.shape), _const_spec(wf.shape),
                  _const_spec(bfg.shape)] + [_const_spec(t.shape) for t in consts],
        out_specs=out_specs,
        out_shape=out_shape,
        scratch_shapes=[pltpu.VMEM((1, LANES), F32)],
        compiler_params=pltpu.CompilerParams(dimension_semantics=("arbitrary", "arbitrary"),
                                             vmem_limit_bytes=VMEM_LIMIT),
        name="mixer_proj",
    )(x2d, mod, g_pre, pos3, invf, wqkv, wf, bfg, *consts)


DIL_AHEAD = 6
PREP_ROWS = 512


def _dil_consts(seq):
    a = np.arange(BLK)[:, None]
    j = np.arange(2 * BLK)[None, :]
    band_ok = np.where(j < BLK, j >= a, j - BLK <= a)
    causal_ok = np.arange(BLK)[None, :] <= a
    n4 = seq // 4
    dist = (n4 - BLK) + a - np.arange(n4)[None, :]
    count = ((dist >= 0) & (dist <= BLK)).astype(np.int32) + ((dist >= 0) & (dist % 4 == 0)).astype(np.int32)
    joint = np.where(count == 0, NEG, np.log2(np.maximum(count, 1)))
    bias = lambda ok: np.where(ok, 0.0, NEG)
    return [t.astype(np.float32) for t in (bias(band_ok), bias(causal_ok), joint)]


def _dil_kernel(q_ref, k_ref, v_ref, band_ref, causal_ref, joint_ref, o_ref, qp, kp, vp, o_sc, l_sc):
    S = q_ref.shape[0]
    nblk = S // BLK
    n4 = S // 4
    first_head = lax.broadcasted_iota(jnp.int32, (BLK, LANES), 1) < HEAD_DIM
    dn = (((1,), (1,)), ((), ()))

    for ref, dst in ((q_ref, qp), (k_ref, kp), (v_ref, vp)):
        for r in range(4):
            dst[1, pl.ds(r * n4, n4), :] = ref[pl.ds(r, n4, stride=4), :].astype(BF16)
        for c0 in range(0, S, PREP_ROWS):
            dst[0, pl.ds(c0, PREP_ROWS), :] = ref[pl.ds(c0, PREP_ROWS), :].astype(BF16)

    tasks = [(p, ib) for p in (1, 0) for ib in range(nblk)]

    def keys_of(p, ib):
        if p == 1:
            cls, jb = divmod(ib, n4 // BLK)
            n = (jb + 1) * BLK
            return pl.ds(cls * n4, n), joint_ref[:, n4 - n:n4]
        if ib == 0:
            return pl.ds(0, BLK), causal_ref[...]
        return pl.ds((ib - 1) * BLK, 2 * BLK), band_ref[...]

    def scores(p, ib):
        q = qp[p, pl.ds(ib * BLK, BLK), :]
        zero = jnp.zeros_like(q)
        q2 = jnp.concatenate([jnp.where(first_head, q, zero), jnp.where(first_head, zero, q)], axis=0)
        keys, bias = keys_of(p, ib)
        s = lax.dot_general(q2, kp[p, keys, :], dn, preferred_element_type=F32)
        ms, ls, es = [], [], []
        for hd in range(2):
            sh = s[hd * BLK:(hd + 1) * BLK, :] + bias
            m = jnp.max(sh, axis=-1, keepdims=True)
            e = jnp.exp2(sh - m)
            ms.append(m)
            ls.append(jnp.sum(e, axis=-1, keepdims=True))
            es.append(e.astype(BF16))
        return ms, ls, jnp.concatenate(es, axis=0)

    def finish(p, ib, ms, ls, acc):
        num = jnp.where(first_head, acc[0:BLK, :], acc[BLK:2 * BLK, :])
        den = jnp.where(first_head, ls[0], ls[1])
        o = num / den
        lse = jnp.where(first_head, ms[0], ms[1]) + jnp.log2(den)
        if p == 1:
            cls, t0 = divmod(ib * BLK, n4)
            dst = pl.ds(cls + t0 * 4, BLK, stride=4)
            o_sc[dst, :] = o
            l_sc[dst, :] = lse
        else:
            rows = pl.ds(ib * BLK, BLK)
            o2, lse2 = o_sc[rows, :], l_sc[rows, :]
            mx = jnp.maximum(lse, lse2)
            w, w2 = jnp.exp2(lse - mx), jnp.exp2(lse2 - mx)
            o_ref[rows, :] = ((w * o + w2 * o2) / (w + w2)).astype(o_ref.dtype)

    pending = {}
    for i in range(len(tasks) + DIL_AHEAD):
        if i < len(tasks):
            pending[i] = scores(*tasks[i])
        j = i - DIL_AHEAD
        if j >= 0:
            p, ib = tasks[j]
            ms, ls, e2 = pending.pop(j)
            acc = jnp.dot(e2, vp[p, keys_of(p, ib)[0], :], preferred_element_type=F32)
            finish(p, ib, ms, ls, acc)


def _dil_call(qa, ka, va, *, batch, seq):
    npair = WIDTH_A // LANES
    spec = pl.BlockSpec((None, None, seq, LANES), lambda b, j: (b, j, 0, 0))
    consts = [jnp.asarray(t, F32) for t in _dil_consts(seq)]
    return pl.pallas_call(
        _dil_kernel,
        grid=(batch, npair),
        in_specs=[spec, spec, spec] + [_const_spec(t.shape) for t in consts],
        out_specs=spec,
        out_shape=jax.ShapeDtypeStruct((batch, npair, seq, LANES), BF16),
        scratch_shapes=[pltpu.VMEM((2, seq, LANES), BF16), pltpu.VMEM((2, seq, LANES), BF16),
                        pltpu.VMEM((2, seq, LANES), BF16),
                        pltpu.VMEM((seq, LANES), F32), pltpu.VMEM((seq, LANES), F32)],
        compiler_params=pltpu.CompilerParams(dimension_semantics=("arbitrary", "arbitrary"),
                                             vmem_limit_bytes=VMEM_LIMIT),
        name="dilated_attn",
    )(qa, ka, va, *consts)


FOX_T = 256
FOX_AHEAD = 3


def _fox_kernel(q_ref, k_ref, v_ref, aq_ref, ak_ref, o_ref, vaug):
    S = q_ref.shape[0]
    nq = S // FOX_T
    pair = pl.program_id(1)
    lane = lax.broadcasted_iota(jnp.int32, (FOX_T, LANES), 1)
    first_head = lane < HEAD_DIM
    row = lax.broadcasted_iota(jnp.int32, (FOX_T, FOX_T), 0)
    col = lax.broadcasted_iota(jnp.int32, (FOX_T, FOX_T), 1)
    causal = col <= row
    dn = (((1,), (1,)), ((), ()))

    fh = lax.broadcasted_iota(jnp.int32, (PREP_ROWS, LANES), 1) < HEAD_DIM
    one = jnp.ones((PREP_ROWS, LANES), BF16)
    for c0 in range(0, S, PREP_ROWS):
        v = v_ref[c0:c0 + PREP_ROWS, :]
        vaug[0, c0:c0 + PREP_ROWS, :] = jnp.where(fh, v, one)
        vaug[1, c0:c0 + PREP_ROWS, :] = jnp.where(fh, one, v)

    tasks = [(qi, hd) for qi in range(nq) for hd in range(2)]

    def scores(qi, hd):
        r0, n = qi * FOX_T, (qi + 1) * FOX_T
        q = q_ref[r0:r0 + FOX_T, :]
        aq = aq_ref[r0:r0 + FOX_T, :]
        lo = AUG * (2 * pair + hd)
        own = (lane >= lo) & (lane < lo + AUG)
        ql = jnp.concatenate(
            [jnp.where(first_head if hd == 0 else jnp.logical_not(first_head), q, jnp.zeros_like(q)),
             jnp.where(own, aq, jnp.zeros_like(aq))], axis=1)
        kk = jnp.concatenate([k_ref[0:n, :], ak_ref[0:n, :]], axis=1)
        s = lax.dot_general(ql, kk, dn, preferred_element_type=F32)
        tiles = [s[:, c:c + LANES] for c in range(0, n - FOX_T, LANES)]
        diag = jnp.where(causal, s[:, n - FOX_T:n], NEG)
        tiles += [diag[:, c:c + LANES] for c in range(0, FOX_T, LANES)]
        mx = tiles[0]
        for t in tiles[1:]:
            mx = jnp.maximum(mx, t)
        m = jnp.max(mx, axis=-1, keepdims=True)
        return jnp.concatenate([jnp.exp2(t - m).astype(BF16) for t in tiles], axis=1)

    pending = {}
    for i in range(len(tasks) + FOX_AHEAD):
        if i < len(tasks):
            pending[i] = scores(*tasks[i])
        j = i - FOX_AHEAD
        if j >= 0:
            qi, hd = tasks[j]
            n = (qi + 1) * FOX_T
            acc = jnp.dot(pending.pop(j), vaug[hd, 0:n, :], preferred_element_type=F32)
            if hd == 0:
                a0 = acc
            else:
                num = jnp.where(first_head, a0, acc)
                den = pltpu.roll(jnp.where(first_head, acc, a0), HEAD_DIM, 1)
                o_ref[qi * FOX_T:(qi + 1) * FOX_T, :] = (num / den).astype(o_ref.dtype)


def _fox_call(qb, kb, vb, aq, ak, *, batch, seq):
    npair = WIDTH_B // LANES
    spec = pl.BlockSpec((None, None, seq, LANES), lambda b, j: (b, j, 0, 0))
    aspec = pl.BlockSpec((None, seq, LANES), lambda b, j: (b, 0, 0))
    return pl.pallas_call(
        _fox_kernel,
        grid=(batch, npair),
        in_specs=[spec, spec, spec, aspec, aspec],
        out_specs=spec,
        out_shape=jax.ShapeDtypeStruct((batch, npair, seq, LANES), BF16),
        scratch_shapes=[pltpu.VMEM((2, seq, LANES), BF16)],
        compiler_params=pltpu.CompilerParams(dimension_semantics=("arbitrary", "arbitrary"),
                                             vmem_limit_bytes=VMEM_LIMIT),
        name="forget_attn",
    )(qb, kb, vb, aq.reshape(batch, seq, LANES), ak.reshape(batch, seq, LANES))


def _padcast_kernel(w_ref, o_ref):
    rows, cols = w_ref.shape
    o_ref[...] = jnp.zeros_like(o_ref)
    o_ref[0:rows, 0:cols] = w_ref[...].astype(o_ref.dtype)


def _padcast(w, shape):
    return pl.pallas_call(
        _padcast_kernel,
        out_shape=jax.ShapeDtypeStruct(shape, BF16),
        compiler_params=pltpu.CompilerParams(vmem_limit_bytes=VMEM_LIMIT),
        name="weight_prep",
    )(w)


def _ffn_weights(w_gate, w_up, w_down):
    return (_padcast(w_gate, (D_MODEL, D_FF_PAD)), _padcast(w_up, (D_MODEL, D_FF_PAD)),
            _padcast(w_down, (D_FF_PAD, D_MODEL)))


def kernel(x, c, positions, w_ada, b_ada, g_pre_ff1, g_post_ff1, w_ff1_gate, w_ff1_up, w_ff1_down,
           g_pre_mix, g_post_mix, w_in, b_forget, g_out_a, g_out_b, w_out,
           g_pre_ff2, g_post_ff2, w_ff2_gate, w_ff2_up, w_ff2_down):
    B, S, D = x.shape
    depth = w_ada.shape[0]
    tm = 2 * PROJ_SUB
    e_np, ec_np = _rope_consts()
    pq, pk, cq, ck = _forget_consts()
    tri = np.tril(np.ones((PROJ_SUB, PROJ_SUB), np.float32))
    consts = (jnp.asarray(e_np, BF16), jnp.asarray(ec_np, F32), jnp.asarray(tri, BF16),
              jnp.asarray(pq, BF16), jnp.asarray(pk, BF16), jnp.asarray(cq, F32), jnp.asarray(ck, F32))
    inv_freq = ROPE_THETA ** (-jnp.arange(0, ROT_DIM, 2, dtype=F32) / ROT_DIM)
    invf = inv_freq.reshape(ROT_DIM // 2, 1)
    pos3 = positions.reshape(B, 1, S)
    x2d = x.reshape(B * S, D)
    row = lambda g: g.reshape(1, -1)

    for l in range(depth):
        mod = _ada_call(c, w_ada[l], b_ada[l]).reshape(B, N_MOD, D)
        wg, wu, wd = _ffn_weights(w_ff1_gate[l], w_ff1_up[l], w_ff1_down[l])
        x2d = _ffn_call(x2d, mod, row(g_pre_ff1[l]), row(g_post_ff1[l]), wg, wu, wd, row0=0, seq=S, tm=2 * FFN_SUB)

        wqkv = w_in[l].astype(BF16)
        wf = jnp.pad(jnp.tile(w_in[l][:, QKV_COLS:], (1, 3)), ((0, 0), (0, LANES - 3 * N_HEADS_B))).astype(BF16)
        bfg = jnp.pad(jnp.tile(b_forget[l], 3), (0, LANES - 3 * N_HEADS_B)).reshape(1, LANES)
        qa, ka, va, qb, kb, vb, aq, ak = _proj_call(
            x2d, mod, row(g_pre_mix[l]), pos3, invf, wqkv, wf, bfg, consts, batch=B, seq=S, tm=tm)
        oa = _dil_call(qa, ka, va, batch=B, seq=S)
        ob = _fox_call(qb, kb, vb, aq, ak, batch=B, seq=S)
        wg, wu, wd = _ffn_weights(w_ff2_gate[l], w_ff2_up[l], w_ff2_down[l])
        mix = (oa, ob, row(g_out_a[l]), row(g_out_b[l]), w_out[l].astype(BF16), row(g_post_mix[l]))
        x2d = _ffn_call(x2d, mod, row(g_pre_ff2[l]), row(g_post_ff2[l]), wg, wu, wd, row0=6, seq=S,
                        tm=2 * FFN_SUB, mix=mix)
    return x2d.reshape(B, S, D)
```

```python
import functools
import math

import numpy as np
import jax
import jax.numpy as jnp
from jax import lax
from jax.experimental import pallas as pl
from jax.experimental.pallas import tpu as pltpu

F32 = jnp.float32
BF16 = jnp.bfloat16

D_MODEL = 1024
HEAD_DIM = 64
N_HEADS_A = 8
N_HEADS_B = 8
WIDTH_A = N_HEADS_A * HEAD_DIM
WIDTH_B = N_HEADS_B * HEAD_DIM
ROT_DIM = HEAD_DIM // 4
ROPE_THETA = 500000.0
D_FF = 2752
N_MOD = 9
EPS = 1e-6
ATTN_SCALE = HEAD_DIM ** -0.5
NEG = -1e30
QKV_COLS = 3 * WIDTH_A + 3 * WIDTH_B
LOG2E = math.log2(math.e)
Q_SCALE = ATTN_SCALE * LOG2E

LANES = 128
FF_CHUNK = 256
D_FF_PAD = -(-D_FF // FF_CHUNK) * FF_CHUNK
N_FF_CHUNKS = D_FF_PAD // FF_CHUNK
FFN_SUB = 512
PROJ_SUB = 512
BLK = 128
AUG = 16
VMEM_LIMIT = 56 * 1024 * 1024


def _rms(x, g):
    ms = jnp.mean(x * x, axis=-1, keepdims=True)
    return x * lax.rsqrt(ms + EPS) * g


def _split3(x):
    hi = x.astype(BF16).astype(F32)
    r1 = x - hi
    mid = r1.astype(BF16).astype(F32)
    lo = (r1 - mid).astype(BF16).astype(F32)
    return hi, mid, lo


def _const_spec(shape):
    nd = len(shape)
    return pl.BlockSpec(shape, lambda *_: (0,) * nd, pipeline_mode=pl.Buffered(1))


def _ada_kernel(c_ref, w_ref, b_ref, o_ref):
    c = c_ref[...]
    sc = (c * jax.nn.sigmoid(c)).astype(BF16)
    o_ref[...] = jnp.dot(sc, w_ref[...].astype(BF16), preferred_element_type=F32) + b_ref[...]


def _ada_call(c, w_ada, b_ada):
    B, D = c.shape
    N = w_ada.shape[1]
    tn = 1024
    return pl.pallas_call(
        _ada_kernel,
        grid=(N // tn,),
        in_specs=[pl.BlockSpec((B, D), lambda j: (0, 0)),
                  pl.BlockSpec((D, tn), lambda j: (0, j)),
                  pl.BlockSpec((1, tn), lambda j: (0, j))],
        out_specs=pl.BlockSpec((B, tn), lambda j: (0, j)),
        out_shape=jax.ShapeDtypeStruct((B, N), F32),
        compiler_params=pltpu.CompilerParams(dimension_semantics=("arbitrary",),
                                             vmem_limit_bytes=VMEM_LIMIT),
        name="adaln",
    )(c, w_ada, b_ada.reshape(1, N))


def _ffn_kernel(*refs, row0, with_mix):
    if with_mix:
        oa_ref, ob_ref, ga_ref, gb_ref, wo_ref, gmix_ref = refs[:6]
        refs = refs[6:]
    x_ref, mod_ref, gpre_ref, gpost_ref, wg_ref, wu_ref, wd_ref, o_ref, a_ref = refs
    shift = mod_ref[0, row0:row0 + 1, :]
    gs = gpre_ref[...] * (1.0 + mod_ref[0, row0 + 1:row0 + 2, :])
    gp = gpost_ref[...] * (0.5 * mod_ref[0, row0 + 2:row0 + 3, :])
    subs = [pl.ds(t * FFN_SUB, FFN_SUB) for t in range(x_ref.shape[0] // FFN_SUB)]
    xs = [x_ref[rows, :] for rows in subs]

    if with_mix:
        gm = gmix_ref[...] * mod_ref[0, 5:6, :]
        for t, rows in enumerate(subs):
            oa = jnp.concatenate([oa_ref[i, rows, :] for i in range(oa_ref.shape[0])], axis=1).astype(F32)
            ob = jnp.concatenate([ob_ref[i, rows, :] for i in range(ob_ref.shape[0])], axis=1).astype(F32)
            a = _rms(oa, ga_ref[...]).astype(BF16)
            b = _rms(ob, gb_ref[...]).astype(BF16)
            y = (jnp.dot(a, wo_ref[0:WIDTH_A, :], preferred_element_type=F32)
                 + jnp.dot(b, wo_ref[WIDTH_A:WIDTH_A + WIDTH_B, :], preferred_element_type=F32))
            xs[t] = xs[t] + _rms(y, gm)

    for t, rows in enumerate(subs):
        x = xs[t]
        ms = jnp.mean(x * x, axis=-1, keepdims=True)
        hb = (x * lax.rsqrt(ms + EPS) * gs + shift).astype(BF16)
        for c in range(N_FF_CHUNKS):
            cs = slice(c * FF_CHUNK, (c + 1) * FF_CHUNK)
            g = jnp.dot(hb, wg_ref[:, cs], preferred_element_type=F32)
            u = jnp.dot(hb, wu_ref[:, cs], preferred_element_type=F32)
            a_ref[t, :, cs] = (g * jax.nn.sigmoid(g) * u).astype(BF16)
        y = jnp.dot(a_ref[t], wd_ref[...], preferred_element_type=F32)
        o_ref[rows, :] = x + _rms(y, gp)


def _ffn_call(x2d, mod, g_pre, g_post, wg, wu, wd, *, row0, seq, tm, mix=None):
    T, D = x2d.shape
    per_seq = seq // tm
    kern = functools.partial(_ffn_kernel, row0=row0, with_mix=mix is not None)
    args, specs = [], []
    if mix is not None:
        oa, ob, ga, gb, wo, gmix = mix
        pair_spec = lambda t: pl.BlockSpec((None, t.shape[1], tm, LANES),
                                           lambda i: (i // per_seq, 0, i % per_seq, 0))
        args += [oa, ob, ga, gb, wo, gmix]
        specs += [pair_spec(oa), pair_spec(ob), _const_spec(ga.shape), _const_spec(gb.shape),
                  _const_spec(wo.shape), _const_spec(gmix.shape)]
    args += [x2d, mod, g_pre, g_post, wg, wu, wd]
    specs += [pl.BlockSpec((tm, D), lambda i: (i, 0)),
              pl.BlockSpec((1, N_MOD, D), lambda i: (i // per_seq, 0, 0)),
              _const_spec((1, D)), _const_spec((1, D)),
              _const_spec(wg.shape), _const_spec(wu.shape), _const_spec(wd.shape)]
    return pl.pallas_call(
        kern,
        grid=(T // tm,),
        in_specs=specs,
        out_specs=pl.BlockSpec((tm, D), lambda i: (i, 0)),
        out_shape=jax.ShapeDtypeStruct((T, D), F32),
        scratch_shapes=[pltpu.VMEM((tm // FFN_SUB, FFN_SUB, D_FF_PAD), BF16)],
        compiler_params=pltpu.CompilerParams(dimension_semantics=("arbitrary",),
                                             vmem_limit_bytes=VMEM_LIMIT),
        name="ffn_row%d" % row0,
    )(*args)


def _rope_consts():
    half = ROT_DIM // 2
    e = np.zeros((3, 2 * half, 3 * LANES), np.float32)
    c = np.zeros((1, 3 * LANES), np.float32)
    for lane in range(LANES):
        d = lane % HEAD_DIM
        if d < ROT_DIM:
            e[:, d % half, lane] = 1.0
        else:
            c[0, lane] = 1.0
        if d < half:
            e[:, half + d, LANES + lane] = -1.0
        elif d < ROT_DIM:
            e[:, half + d - half, 2 * LANES + lane] = 1.0
    return e.reshape(3 * 2 * half, 3 * LANES), c


def _forget_consts():
    pq = np.zeros((LANES, LANES), np.float32)
    pk = np.zeros((LANES, LANES), np.float32)
    cq = np.zeros((1, LANES), np.float32)
    ck = np.zeros((1, LANES), np.float32)
    for h in range(N_HEADS_B):
        for g in range(3):
            pq[8 * g + h, AUG * h + g] = 1.0
            cq[0, AUG * h + 3 + g] = 1.0
            pk[8 * g + h, AUG * h + 3 + g] = -1.0
            ck[0, AUG * h + g] = 1.0
    return pq, pk, cq, ck


def _proj_kernel(x_ref, mod_ref, g_ref, pos_ref, invf_ref, wqkv_ref, wf_ref, bf_ref, e_ref, ec_ref, tri_ref,
                 pq_ref, pk_ref, cq_ref, ck_ref,
                 qa_ref, ka_ref, va_ref, qb_ref, kb_ref, vb_ref, aq_ref, ak_ref, carry_ref):
    @pl.when(pl.program_id(1) == 0)
    def _():
        carry_ref[...] = jnp.zeros_like(carry_ref)

    shift = mod_ref[0, 3:4, :]
    gs = g_ref[...] * (1.0 + mod_ref[0, 4:5, :])
    half = ROT_DIM // 2
    carry = carry_ref[...]
    for t in range(x_ref.shape[0] // PROJ_SUB):
        rows = pl.ds(t * PROJ_SUB, PROJ_SUB)
        x = x_ref[rows, :]
        ms = jnp.mean(x * x, axis=-1, keepdims=True)
        hb = (x * lax.rsqrt(ms + EPS) * gs + shift).astype(BF16)

        def cols(c0, width, hb=hb):
            return jnp.dot(hb, wqkv_ref[:, c0:c0 + width], preferred_element_type=F32)

        def put_pairs(ref, u, fn=lambda v: v, rows=rows):
            for i in range(u.shape[1] // LANES):
                ref[i, rows, :] = fn(u[:, i * LANES:(i + 1) * LANES]).astype(ref.dtype)


        fl = jnp.dot(hb, wf_ref[...], preferred_element_type=F32) + bf_ref[...]
        put_pairs(va_ref, cols(2 * WIDTH_A, WIDTH_A))

        posf = pos_ref[:, t * PROJ_SUB:(t + 1) * PROJ_SUB].astype(F32)
        ang = invf_ref[...] * posf
        cs = jnp.concatenate([jnp.cos(ang), jnp.sin(ang)], axis=0)
        cs3 = jnp.concatenate([u.astype(BF16) for u in _split3(cs)], axis=0)
        tabs = lax.dot_general(cs3, e_ref[...], (((0,), (0,)), ((), ())),
                               preferred_element_type=F32) + ec_ref[...]
        t_cos = tabs[:, 0:LANES]
        t_s1 = tabs[:, LANES:2 * LANES]
        t_s2 = tabs[:, 2 * LANES:3 * LANES]

        def rope(u, t_cos=t_cos, t_s1=t_s1, t_s2=t_s2):
            return u * t_cos + pltpu.roll(u, LANES - half, 1) * t_s1 + pltpu.roll(u, half, 1) * t_s2

        lf = jnp.minimum(fl, 0.0) - jnp.log1p(jnp.exp(-jnp.abs(fl)))
        lane = lax.broadcasted_iota(jnp.int32, lf.shape, 1)
        zero = jnp.zeros_like(lf)

        def by_group(a, b, c, lane=lane, zero=zero):
            return jnp.where(lane < 8, a, jnp.where(lane < 16, b, jnp.where(lane < 24, c, zero))).astype(BF16)

        cum = jnp.dot(tri_ref[...], by_group(*_split3(lf)), preferred_element_type=F32)

        put_pairs(qa_ref, cols(0, WIDTH_A), lambda u: rope(u) * Q_SCALE)

        cum = cum + pltpu.roll(cum, LANES - 8, 1) + pltpu.roll(cum, LANES - 16, 1)
        f_tot = cum + carry
        carry = f_tot[PROJ_SUB - 1:PROJ_SUB, :]
        p1, p2, p3 = _split3(f_tot * LOG2E)
        pieces = by_group(p1, pltpu.roll(p2, 8, 1), pltpu.roll(p3, 16, 1))
        aq_ref[rows, :] = (jnp.dot(pieces, pq_ref[...], preferred_element_type=F32) + cq_ref[...]).astype(BF16)
        ak_ref[rows, :] = (jnp.dot(pieces, pk_ref[...], preferred_element_type=F32) + ck_ref[...]).astype(BF16)

        put_pairs(ka_ref, cols(WIDTH_A, WIDTH_A), rope)
        o = 3 * WIDTH_A
        put_pairs(qb_ref, cols(o, WIDTH_B), lambda u: u * Q_SCALE)
        put_pairs(kb_ref, cols(o + WIDTH_B, WIDTH_B))
        put_pairs(vb_ref, cols(o + 2 * WIDTH_B, WIDTH_B))
    carry_ref[...] = carry


def _proj_call(x2d, mod, g_pre, pos3, invf, wqkv, wf, bfg, consts, *, batch, seq, tm):
    T, D = x2d.shape
    per_seq = seq // tm
    tok = lambda b, j: (b * per_seq + j, 0)
    npa, npb = WIDTH_A // LANES, WIDTH_B // LANES
    pairs = lambda b, j: (b, 0, j, 0)
    out_shape = ([jax.ShapeDtypeStruct((batch, npa, seq, LANES), F32)] * 3
                 + [jax.ShapeDtypeStruct((batch, npb, seq, LANES), BF16)] * 3
                 + [jax.ShapeDtypeStruct((T, LANES), BF16)] * 2)
    out_specs = ([pl.BlockSpec((None, npa, tm, LANES), pairs)] * 3
                 + [pl.BlockSpec((None, npb, tm, LANES), pairs)] * 3
                 + [pl.BlockSpec((tm, LANES), tok)] * 2)
    return pl.pallas_call(
        _proj_kernel,
        grid=(batch, per_seq),
        in_specs=[pl.BlockSpec((tm, D), tok),
                  pl.BlockSpec((1, N_MOD, D), lambda b, j: (b, 0, 0)),
                  _const_spec((1, D)),
                  pl.BlockSpec((None, 1, tm), lambda b, j: (b, 0, j)),
                  _const_spec(invf.shape), _const_spec(wqkv.shape), _const_spec(wf.shape),
                  _const_spec(bfg.shape)] + [_const_spec(t.shape) for t in consts],
        out_specs=out_specs,
        out_shape=out_shape,
        scratch_shapes=[pltpu.VMEM((1, LANES), F32)],
        compiler_params=pltpu.CompilerParams(dimension_semantics=("arbitrary", "arbitrary"),
                                             vmem_limit_bytes=VMEM_LIMIT),
        name="mixer_proj",
    )(x2d, mod, g_pre, pos3, invf, wqkv, wf, bfg, *consts)


DIL_AHEAD = 6
PREP_ROWS = 512


def _dil_consts(seq):
    a = np.arange(BLK)[:, None]
    j = np.arange(2 * BLK)[None, :]
    band_ok = np.where(j < BLK, j >= a, j - BLK <= a)
    causal_ok = np.arange(BLK)[None, :] <= a
    n4 = seq // 4
    dist = (n4 - BLK) + a - np.arange(n4)[None, :]
    count = ((dist >= 0) & (dist <= BLK)).astype(np.int32) + ((dist >= 0) & (dist % 4 == 0)).astype(np.int32)
    joint = np.where(count == 0, NEG, np.log2(np.maximum(count, 1)))
    bias = lambda ok: np.where(ok, 0.0, NEG)
    return [t.astype(np.float32) for t in (bias(band_ok), bias(causal_ok), joint)]


def _dil_kernel(q_ref, k_ref, v_ref, band_ref, causal_ref, joint_ref, o_ref, qp, kp, vp, o_sc, l_sc):
    S = q_ref.shape[0]
    nblk = S // BLK
    n4 = S // 4
    first_head = lax.broadcasted_iota(jnp.int32, (BLK, LANES), 1) < HEAD_DIM
    dn = (((1,), (1,)), ((), ()))

    for ref, dst in ((q_ref, qp), (k_ref, kp), (v_ref, vp)):
        for r in range(4):
            dst[1, pl.ds(r * n4, n4), :] = ref[pl.ds(r, n4, stride=4), :].astype(BF16)
        for c0 in range(0, S, PREP_ROWS):
            dst[0, pl.ds(c0, PREP_ROWS), :] = ref[pl.ds(c0, PREP_ROWS), :].astype(BF16)

    per_cls = n4 // BLK
    tasks = []
    for jb in range(per_cls):
        tasks += [(1, cls * per_cls + jb) for cls in range(4)]
        tasks += [(0, ib) for ib in range(jb * nblk // per_cls, (jb + 1) * nblk // per_cls)]

    def keys_of(p, ib):
        if p == 1:
            cls, jb = divmod(ib, n4 // BLK)
            n = (jb + 1) * BLK
            return pl.ds(cls * n4, n), joint_ref[:, n4 - n:n4]
        if ib == 0:
            return pl.ds(0, BLK), causal_ref[...]
        return pl.ds((ib - 1) * BLK, 2 * BLK), band_ref[...]

    def scores(p, ib):
        q = qp[p, pl.ds(ib * BLK, BLK), :]
        zero = jnp.zeros_like(q)
        q2 = jnp.concatenate([jnp.where(first_head, q, zero), jnp.where(first_head, zero, q)], axis=0)
        keys, bias = keys_of(p, ib)
        s = lax.dot_general(q2, kp[p, keys, :], dn, preferred_element_type=F32)
        ms, ls, es = [], [], []
        for hd in range(2):
            sh = s[hd * BLK:(hd + 1) * BLK, :] + bias
            m = jnp.max(sh, axis=-1, keepdims=True)
            e = jnp.exp2(sh - m)
            ms.append(m)
            ls.append(jnp.sum(e, axis=-1, keepdims=True))
            es.append(e.astype(BF16))
        return ms, ls, jnp.concatenate(es, axis=0)

    def finish(p, ib, ms, ls, acc):
        num = jnp.where(first_head, acc[0:BLK, :], acc[BLK:2 * BLK, :])
        den = jnp.where(first_head, ls[0], ls[1])
        o = num / den
        lse = jnp.where(first_head, ms[0], ms[1]) + jnp.log2(den)
        if p == 1:
            cls, t0 = divmod(ib * BLK, n4)
            dst = pl.ds(cls + t0 * 4, BLK, stride=4)
            o_sc[dst, :] = o
            l_sc[dst, :] = lse
        else:
            rows = pl.ds(ib * BLK, BLK)
            o2, lse2 = o_sc[rows, :], l_sc[rows, :]
            mx = jnp.maximum(lse, lse2)
            w, w2 = jnp.exp2(lse - mx), jnp.exp2(lse2 - mx)
            o_ref[rows, :] = ((w * o + w2 * o2) / (w + w2)).astype(o_ref.dtype)

    pending = {}
    for i in range(len(tasks) + DIL_AHEAD):
        if i < len(tasks):
            pending[i] = scores(*tasks[i])
        j = i - DIL_AHEAD
        if j >= 0:
            p, ib = tasks[j]
            ms, ls, e2 = pending.pop(j)
            acc = jnp.dot(e2, vp[p, keys_of(p, ib)[0], :], preferred_element_type=F32)
            finish(p, ib, ms, ls, acc)


def _dil_call(qa, ka, va, *, batch, seq):
    npair = WIDTH_A // LANES
    spec = pl.BlockSpec((None, None, seq, LANES), lambda b, j: (b, j, 0, 0))
    consts = [jnp.asarray(t, F32) for t in _dil_consts(seq)]
    return pl.pallas_call(
        _dil_kernel,
        grid=(batch, npair),
        in_specs=[spec, spec, spec] + [_const_spec(t.shape) for t in consts],
        out_specs=spec,
        out_shape=jax.ShapeDtypeStruct((batch, npair, seq, LANES), BF16),
        scratch_shapes=[pltpu.VMEM((2, seq, LANES), BF16), pltpu.VMEM((2, seq, LANES), BF16),
                        pltpu.VMEM((2, seq, LANES), BF16),
                        pltpu.VMEM((seq, LANES), F32), pltpu.VMEM((seq, LANES), F32)],
        compiler_params=pltpu.CompilerParams(dimension_semantics=("arbitrary", "arbitrary"),
                                             vmem_limit_bytes=VMEM_LIMIT),
        name="dilated_attn",
    )(qa, ka, va, *consts)


FOX_T = 256
FOX_AHEAD = 3


def _fox_kernel(q_ref, k_ref, v_ref, aq_ref, ak_ref, o_ref, vaug):
    S = q_ref.shape[0]
    nq = S // FOX_T
    pair = pl.program_id(1)
    lane = lax.broadcasted_iota(jnp.int32, (FOX_T, LANES), 1)
    first_head = lane < HEAD_DIM
    row = lax.broadcasted_iota(jnp.int32, (FOX_T, FOX_T), 0)
    col = lax.broadcasted_iota(jnp.int32, (FOX_T, FOX_T), 1)
    causal = col <= row
    dn = (((1,), (1,)), ((), ()))

    fh = lax.broadcasted_iota(jnp.int32, (PREP_ROWS, LANES), 1) < HEAD_DIM
    one = jnp.ones((PREP_ROWS, LANES), BF16)
    for c0 in range(0, S, PREP_ROWS):
        v = v_ref[c0:c0 + PREP_ROWS, :]
        vaug[0, c0:c0 + PREP_ROWS, :] = jnp.where(fh, v, one)
        vaug[1, c0:c0 + PREP_ROWS, :] = jnp.where(fh, one, v)

    tasks = [(qi, hd) for qi in range(nq) for hd in range(2)]

    def scores(qi, hd):
        r0, n = qi * FOX_T, (qi + 1) * FOX_T
        q = q_ref[r0:r0 + FOX_T, :]
        aq = aq_ref[r0:r0 + FOX_T, :]
        lo = AUG * (2 * pair + hd)
        own = (lane >= lo) & (lane < lo + AUG)
        ql = jnp.concatenate(
            [jnp.where(first_head if hd == 0 else jnp.logical_not(first_head), q, jnp.zeros_like(q)),
             jnp.where(own, aq, jnp.zeros_like(aq))], axis=1)
        kk = jnp.concatenate([k_ref[0:n, :], ak_ref[0:n, :]], axis=1)
        s = lax.dot_general(ql, kk, dn, preferred_element_type=F32)
        tiles = [s[:, c:c + LANES] for c in range(0, n - FOX_T, LANES)]
        diag = jnp.where(causal, s[:, n - FOX_T:n], NEG)
        tiles += [diag[:, c:c + LANES] for c in range(0, FOX_T, LANES)]
        mx = tiles[0]
        for t in tiles[1:]:
            mx = jnp.maximum(mx, t)
        m = jnp.max(mx, axis=-1, keepdims=True)
        return jnp.concatenate([jnp.exp2(t - m).astype(BF16) for t in tiles], axis=1)

    pending = {}
    for i in range(len(tasks) + FOX_AHEAD):
        if i < len(tasks):
            pending[i] = scores(*tasks[i])
        j = i - FOX_AHEAD
        if j >= 0:
            qi, hd = tasks[j]
            n = (qi + 1) * FOX_T
            acc = jnp.dot(pending.pop(j), vaug[hd, 0:n, :], preferred_element_type=F32)
            if hd == 0:
                a0 = acc
            else:
                num = jnp.where(first_head, a0, acc)
                den = pltpu.roll(jnp.where(first_head, acc, a0), HEAD_DIM, 1)
                o_ref[qi * FOX_T:(qi + 1) * FOX_T, :] = (num / den).astype(o_ref.dtype)


def _fox_call(qb, kb, vb, aq, ak, *, batch, seq):
    npair = WIDTH_B // LANES
    spec = pl.BlockSpec((None, None, seq, LANES), lambda b, j: (b, j, 0, 0))
    aspec = pl.BlockSpec((None, seq, LANES), lambda b, j: (b, 0, 0))
    return pl.pallas_call(
        _fox_kernel,
        grid=(batch, npair),
        in_specs=[spec, spec, spec, aspec, aspec],
        out_specs=spec,
        out_shape=jax.ShapeDtypeStruct((batch, npair, seq, LANES), BF16),
        scratch_shapes=[pltpu.VMEM((2, seq, LANES), BF16)],
        compiler_params=pltpu.CompilerParams(dimension_semantics=("arbitrary", "arbitrary"),
                                             vmem_limit_bytes=VMEM_LIMIT),
        name="forget_attn",
    )(qb, kb, vb, aq.reshape(batch, seq, LANES), ak.reshape(batch, seq, LANES))


def _ffn_weights(w_gate, w_up, w_down):
    pad = D_FF_PAD - D_FF
    cols = lambda w: jnp.pad(w.astype(BF16), ((0, 0), (0, pad)))
    wd = jnp.pad(w_down.astype(BF16), ((0, pad), (0, 0)))
    return cols(w_gate), cols(w_up), wd


def kernel(x, c, positions, w_ada, b_ada, g_pre_ff1, g_post_ff1, w_ff1_gate, w_ff1_up, w_ff1_down,
           g_pre_mix, g_post_mix, w_in, b_forget, g_out_a, g_out_b, w_out,
           g_pre_ff2, g_post_ff2, w_ff2_gate, w_ff2_up, w_ff2_down):
    B, S, D = x.shape
    depth = w_ada.shape[0]
    tm = 2 * PROJ_SUB
    e_np, ec_np = _rope_consts()
    pq, pk, cq, ck = _forget_consts()
    tri = np.tril(np.ones((PROJ_SUB, PROJ_SUB), np.float32))
    consts = (jnp.asarray(e_np, BF16), jnp.asarray(ec_np, F32), jnp.asarray(tri, BF16),
              jnp.asarray(pq, BF16), jnp.asarray(pk, BF16), jnp.asarray(cq, F32), jnp.asarray(ck, F32))
    inv_freq = ROPE_THETA ** (-jnp.arange(0, ROT_DIM, 2, dtype=F32) / ROT_DIM)
    invf = inv_freq.reshape(ROT_DIM // 2, 1)
    pos3 = positions.reshape(B, 1, S)
    x2d = x.reshape(B * S, D)
    row = lambda g: g.reshape(1, -1)

    for l in range(depth):
        mod = _ada_call(c, w_ada[l], b_ada[l]).reshape(B, N_MOD, D)
        wg, wu, wd = _ffn_weights(w_ff1_gate[l], w_ff1_up[l], w_ff1_down[l])
        x2d = _ffn_call(x2d, mod, row(g_pre_ff1[l]), row(g_post_ff1[l]), wg, wu, wd, row0=0, seq=S, tm=2 * FFN_SUB)

        wqkv = w_in[l].astype(BF16)
        wf = jnp.pad(jnp.tile(w_in[l][:, QKV_COLS:], (1, 3)), ((0, 0), (0, LANES - 3 * N_HEADS_B))).astype(BF16)
        bfg = jnp.pad(jnp.tile(b_forget[l], 3), (0, LANES - 3 * N_HEADS_B)).reshape(1, LANES)
        qa, ka, va, qb, kb, vb, aq, ak = _proj_call(
            x2d, mod, row(g_pre_mix[l]), pos3, invf, wqkv, wf, bfg, consts, batch=B, seq=S, tm=tm)
        oa = _dil_call(qa, ka, va, batch=B, seq=S)
        ob = _fox_call(qb, kb, vb, aq, ak, batch=B, seq=S)
        wg, wu, wd = _ffn_weights(w_ff2_gate[l], w_ff2_up[l], w_ff2_down[l])
        mix = (oa, ob, row(g_out_a[l]), row(g_out_b[l]), w_out[l].astype(BF16), row(g_post_mix[l]))
        x2d = _ffn_call(x2d, mod, row(g_pre_ff2[l]), row(g_post_ff2[l]), wg, wu, wd, row0=6, seq=S,
                        tm=2 * FFN_SUB, mix=mix)
    return x2d.reshape(B, S, D)
```
